```python
import jax, jax.numpy as jnp
from jax import lax
import numpy as np

D_MODEL = 4096
BATCH = 2
SEQ = 8192
DEPTH = 4
DEC_BATCH = 8
DEC_SEQ = 2048
PAST_LEN = 128

GRID_W = 64
ATT_HEAD_DIM = 128
ATT_WIDTH = D_MODEL // 2
ATT_HEADS = ATT_WIDTH // ATT_HEAD_DIM
ATT_KV_HEADS = ATT_HEADS // 4
ATT_KV_WIDTH = ATT_KV_HEADS * ATT_HEAD_DIM
Q_BLOCK = 128
ROPE_BASE = 10000.0
RWKV_WIDTH = D_MODEL - ATT_WIDTH
RWKV_HEAD_DIM = 64
RWKV_HEADS = RWKV_WIDTH // RWKV_HEAD_DIM
RWKV_CHUNK = 32
DECAY_RANK = 96
ICL_RANK = 96
GATE_RANK = 256
D_FF = 5504
NORM_EPS = 1e-6
GN_EPS = 64e-5
ATT_COLS = ATT_WIDTH + 2 * ATT_KV_WIDTH
RWKV_COLS = 3 * RWKV_WIDTH + 2 * DECAY_RANK + 2 * ICL_RANK + GATE_RANK
IN_COLS = ATT_COLS + RWKV_COLS

kernel_name = 'hybrid_attn_rwkv7_macaron_encoder'


def _split_cols(z, sizes):
    idx = [int(i) for i in np.cumsum(sizes)[:-1]]
    return jnp.split(z, idx, axis=-1)


def _rmsnorm(x, g):
    xf = x.astype(jnp.float32)
    y = xf * lax.rsqrt(jnp.mean(xf * xf, axis=-1, keepdims=True) + NORM_EPS)
    return (y * g.astype(jnp.float32)).astype(x.dtype)


def _swiglu(h, w_gate, w_up, w_down):
    return (jax.nn.silu(h @ w_gate) * (h @ w_up)) @ w_down


def _axial_rope_tables(T):
    rows = T // GRID_W
    row = jnp.repeat(jnp.arange(rows), GRID_W).astype(jnp.float32)
    col = jnp.tile(jnp.arange(GRID_W), rows).astype(jnp.float32)
    half = ATT_HEAD_DIM // 2
    inv = jnp.power(ROPE_BASE, -jnp.arange(0, half, 2, dtype=jnp.float32) / half)
    ang_r = row[:, None] * inv[None, :]
    ang_c = col[:, None] * inv[None, :]
    return (jnp.cos(ang_r), jnp.sin(ang_r), jnp.cos(ang_c), jnp.sin(ang_c))


def _rope_half(x, cos, sin):
    x1, x2 = jnp.split(x, 2, axis=-1)
    c = cos[:, None, :]
    s = sin[:, None, :]
    return jnp.concatenate([x1 * c - x2 * s, x1 * s + x2 * c], axis=-1)


def _axial_rope(x, tabs):
    cr, sr, cc, sc = tabs
    xr, xc = jnp.split(x.astype(jnp.float32), 2, axis=-1)
    return jnp.concatenate([_rope_half(xr, cr, sr), _rope_half(xc, cc, sc)], axis=-1)


def _attention(q, k, v, q_gain, k_gain, tabs):
    B, T, _ = q.shape
    cd = q.dtype
    G = ATT_HEADS // ATT_KV_HEADS
    scale = ATT_HEAD_DIM ** -0.5
    q = q.reshape(B, T, ATT_HEADS, ATT_HEAD_DIM)
    k = k.reshape(B, T, ATT_KV_HEADS, ATT_HEAD_DIM)
    v = v.reshape(B, T, ATT_KV_HEADS, ATT_HEAD_DIM)
    q = (_axial_rope(_rmsnorm(q, q_gain), tabs) * scale).astype(cd)
    k = _axial_rope(_rmsnorm(k, k_gain), tabs).astype(cd)
    nblk = T // Q_BLOCK
    qb = q.reshape(B, nblk, Q_BLOCK, ATT_KV_HEADS, G, ATT_HEAD_DIM).transpose(1, 0, 2, 3, 4, 5)

    def block(qi):
        s = jnp.einsum('bqhgd,bkhd->bhgqk', qi, k, preferred_element_type=jnp.float32)
        p = jax.nn.softmax(s, axis=-1).astype(cd)
        return jnp.einsum('bhgqk,bkhd->bqhgd', p, v, preferred_element_type=jnp.float32).astype(cd)

    o = lax.map(block, qb)
    return o.transpose(1, 0, 2, 3, 4, 5).reshape(B, T, ATT_WIDTH)


def _bi_token_shift(z, mu_prev, mu_next):
    zp = jnp.pad(z, ((0, 0), (1, 0), (0, 0)))[:, :-1]
    zn = jnp.pad(z, ((0, 0), (0, 1), (0, 0)))[:, 1:]
    return z + mu_prev * (zp - z) + mu_next * (zn - z)


def _rwkv7_chunked(r, lw, k, v, a, b, inclusive):
    B, T, H, N = r.shape
    L = RWKV_CHUNK
    nC = T // L

    def blk(t):
        return t.reshape(B, nC, L, H, N).transpose(0, 1, 3, 2, 4)

    r, lw, k, v, a, b = (blk(t) for t in (r, lw, k, v, a, b))
    c = jnp.cumsum(lw, axis=3)
    c_ex = c - lw
    m = c[:, :, :, L // 2 - 1:L // 2, :]
    c_end = c[:, :, :, -1:, :]
    c_rd = c if inclusive else c_ex
    a_q = a * jnp.exp(c_ex - m)
    b_k = b * jnp.exp(m - c)
    k_k = k * jnp.exp(m - c)
    r_q = r * jnp.exp(c_rd - m)
    a_0 = a * jnp.exp(c_ex)
    r_0 = r * jnp.exp(c_rd)
    b_e = b * jnp.exp(c_end - c)
    k_e = k * jnp.exp(c_end - c)
    w_e = jnp.exp(c_end[:, :, :, 0, :])
    idx = jnp.arange(L)
    strict = idx[:, None] > idx[None, :]
    rmask = (idx[:, None] >= idx[None, :]) if inclusive else strict
    A = jnp.where(strict, jnp.einsum('bchtk,bchsk->bchts', a_q, b_k), 0.0)
    Bm = jnp.where(strict, jnp.einsum('bchtk,bchsk->bchts', a_q, k_k), 0.0)
    Mb = jnp.where(rmask, jnp.einsum('bchtk,bchsk->bchts', r_q, b_k), 0.0)
    Mk = jnp.where(rmask, jnp.einsum('bchtk,bchsk->bchts', r_q, k_k), 0.0)
    eye = jnp.eye(L, dtype=jnp.float32)
    inv = lax.linalg.triangular_solve(eye - A, jnp.broadcast_to(eye, A.shape), left_side=True, lower=True)
    P = jnp.einsum('bchts,bchsk->bchtk', inv, a_0)
    Q = jnp.einsum('bchts,bchsv->bchtv', inv, jnp.einsum('bchts,bchsv->bchtv', Bm, v))
    Yl = jnp.einsum('bchts,bchsv->bchtv', Mk, v)
    xs = tuple(jnp.moveaxis(t, 1, 0) for t in (P, Q, r_0, Mb, Yl, b_e, k_e, v, w_e))

    def step(S, inp):
        P_c, Q_c, r0_c, Mb_c, Yl_c, be_c, ke_c, v_c, we_c = inp
        U = jnp.einsum('bhlk,bhvk->bhlv', P_c, S) + Q_c
        Y = jnp.einsum('bhlk,bhvk->bhlv', r0_c, S) + jnp.einsum('bhls,bhsv->bhlv', Mb_c, U) + Yl_c
        S_new = (S * we_c[:, :, None, :] + jnp.einsum('bhsv,bhsk->bhvk', U, be_c)
                 + jnp.einsum('bhsv,bhsk->bhvk', v_c, ke_c))
        return S_new, Y

    S0 = jnp.zeros((B, H, N, N), jnp.float32)
    _, ys = lax.scan(step, S0, xs)
    return ys.transpose(1, 0, 3, 2, 4).reshape(B, T, H, N)


def _rwkv7_mixer(z, w0, w_up, a0, a_up, g_up, k_k, k_a, r_k, ln_g, ln_b):
    B, T, _ = z.shape
    f32 = jnp.float32
    w0, w_up, a0, a_up, g_up, k_k, k_a, r_k, ln_g, ln_b = (
        t.astype(f32) for t in (w0, w_up, a0, a_up, g_up, k_k, k_a, r_k, ln_g, ln_b))
    r, k, v, wd_f, wd_b, ad_f, ad_b, gd = _split_cols(
        z.astype(f32), (RWKV_WIDTH, RWKV_WIDTH, RWKV_WIDTH, DECAY_RANK, DECAY_RANK, ICL_RANK, ICL_RANK, GATE_RANK))

    def heads(t):
        return t.reshape(B, T, RWKV_HEADS, RWKV_HEAD_DIM)

    kk = heads(k * k_k)
    kk = kk * lax.rsqrt(jnp.maximum(jnp.sum(kk * kk, axis=-1, keepdims=True), 1e-24))
    rh = heads(r)
    vh = heads(v)

    def direction(wd, w0_d, w_up_d, ad, a0_d, a_up_d, backward):
        w_log = -jax.nn.softplus(-(w0_d + jnp.tanh(wd) @ w_up_d)) - 0.5
        lw = -jnp.exp(w_log)
        a = jax.nn.sigmoid(a0_d + ad @ a_up_d)
        k_d = k * (1.0 + (a - 1.0) * k_a)
        args = (rh, heads(lw), heads(k_d), vh, -kk, kk * heads(a))
        if backward:
            args = tuple(t[:, ::-1] for t in args)
            y = _rwkv7_chunked(*args, inclusive=False)[:, ::-1]
        else:
            y = _rwkv7_chunked(*args, inclusive=True)
        return y, k_d

    y_f, k_f = direction(wd_f, w0[0], w_up[0], ad_f, a0[0], a_up[0], False)
    y_b, _ = direction(wd_b, w0[1], w_up[1], ad_b, a0[1], a_up[1], True)
    y = y_f + y_b
    mu = jnp.mean(y, axis=-1, keepdims=True)
    var = jnp.mean(jnp.square(y - mu), axis=-1, keepdims=True)
    yn = ((y - mu) * lax.rsqrt(var + GN_EPS)).reshape(B, T, RWKV_WIDTH) * ln_g + ln_b
    bonus = jnp.sum(rh * heads(k_f) * r_k, axis=-1, keepdims=True) * vh
    g = jax.nn.sigmoid(gd) @ g_up
    return ((yn + bonus.reshape(B, T, RWKV_WIDTH)) * g).astype(z.dtype)


def _trunk(x, params):
    (ffn1_norm, ffn1_gate, ffn1_up, ffn1_down, mix_norm, w_in, q_norm, k_norm, shift_mu,
     decay_w0, decay_up, icl_a0, icl_up, gate_up, rwkv_k_k, rwkv_k_a, rwkv_r_k, ln_x_g, ln_x_b,
     w_out, ffn2_norm, ffn2_gate, ffn2_up, ffn2_down, final_norm) = params
    T = x.shape[1]
    tabs = _axial_rope_tables(T)
    for l in range(DEPTH):
        h = _rmsnorm(x, ffn1_norm[l])
        x = x + 0.5 * _swiglu(h, ffn1_gate[l], ffn1_up[l], ffn1_down[l])
        h = _rmsnorm(x, mix_norm[l])
        zin = h @ w_in[l]
        z_att, z_rwkv = _split_cols(zin, (ATT_COLS, RWKV_COLS))
        q, k, v = _split_cols(z_att, (ATT_WIDTH, ATT_KV_WIDTH, ATT_KV_WIDTH))
        att_out = _attention(q, k, v, q_norm[l], k_norm[l], tabs)
        z_rwkv = _bi_token_shift(z_rwkv, shift_mu[l, 0], shift_mu[l, 1])
        rwkv_out = _rwkv7_mixer(z_rwkv, decay_w0[l], decay_up[l], icl_a0[l], icl_up[l], gate_up[l],
                                rwkv_k_k[l], rwkv_k_a[l], rwkv_r_k[l], ln_x_g[l], ln_x_b[l])
        x = x + jnp.concatenate([att_out, rwkv_out], axis=-1) @ w_out[l]
        h = _rmsnorm(x, ffn2_norm[l])
        x = x + 0.5 * _swiglu(h, ffn2_gate[l], ffn2_up[l], ffn2_down[l])
    return _rmsnorm(x, final_norm)


def setup_inputs(seed: int = 0) -> dict:
    key = jax.random.key(seed)
    ks = jax.random.split(key, 28)
    f32 = jnp.float32
    L, D, RW = DEPTH, D_MODEL, RWKV_WIDTH

    def nrm(k, shape, s):
        return jax.random.normal(k, shape, f32) * s

    def gain(k, shape):
        return 1.0 + 0.05 * jax.random.normal(k, shape, f32)

    def unif(k, shape, lo, hi):
        return jax.random.uniform(k, shape, f32, lo, hi)

    return {
        'x_prompt': nrm(ks[0], (BATCH, SEQ, D), 1.0),
        'x_sample': nrm(ks[1], (DEC_BATCH, DEC_SEQ, D), 1.0),
        'ffn1_norm': gain(ks[2], (L, D)),
        'ffn1_gate': nrm(ks[3], (L, D, D_FF), D ** -0.5),
        'ffn1_up': nrm(ks[4], (L, D, D_FF), D ** -0.5),
        'ffn1_down': nrm(ks[5], (L, D_FF, D), D_FF ** -0.5),
        'mix_norm': gain(ks[6], (L, D)),
        'w_in': nrm(ks[7], (L, D, IN_COLS), D ** -0.5),
        'q_norm': gain(ks[8], (L, ATT_HEAD_DIM)),
        'k_norm': gain(ks[9], (L, ATT_HEAD_DIM)),
        'shift_mu': unif(ks[10], (L, 2, RWKV_COLS), 0.0, 0.5),
        'decay_w0': unif(ks[11], (L, 2, RW), -3.0, -1.0),
        'decay_up': nrm(ks[12], (L, 2, DECAY_RANK, RW), 0.5 * DECAY_RANK ** -0.5),
        'icl_a0': nrm(ks[13], (L, 2, RW), 0.5),
        'icl_up': nrm(ks[14], (L, 2, ICL_RANK, RW), 0.5 * ICL_RANK ** -0.5),
        'gate_up': nrm(ks[15], (L, GATE_RANK, RW), GATE_RANK ** -0.5),
        'rwkv_k_k': 0.85 + 0.05 * jax.random.normal(ks[16], (L, RW), f32),
        'rwkv_k_a': gain(ks[17], (L, RW)),
        'rwkv_r_k': nrm(ks[18], (L, RWKV_HEADS, RWKV_HEAD_DIM), 0.1),
        'ln_x_g': gain(ks[19], (L, RW)),
        'ln_x_b': nrm(ks[20], (L, RW), 0.01),
        'w_out': nrm(ks[21], (L, D, D), D ** -0.5),
        'ffn2_norm': gain(ks[22], (L, D)),
        'ffn2_gate': nrm(ks[23], (L, D, D_FF), D ** -0.5),
        'ffn2_up': nrm(ks[24], (L, D, D_FF), D ** -0.5),
        'ffn2_down': nrm(ks[25], (L, D_FF, D), D_FF ** -0.5),
        'final_norm': gain(ks[26], (D,)),
    }


def reference(x_prompt, x_sample, ffn1_norm, ffn1_gate, ffn1_up, ffn1_down, mix_norm, w_in, q_norm, k_norm,
              shift_mu, decay_w0, decay_up, icl_a0, icl_up, gate_up, rwkv_k_k, rwkv_k_a, rwkv_r_k,
              ln_x_g, ln_x_b, w_out, ffn2_norm, ffn2_gate, ffn2_up, ffn2_down, final_norm):
    params = (ffn1_norm, ffn1_gate, ffn1_up, ffn1_down, mix_norm, w_in, q_norm, k_norm, shift_mu,
              decay_w0, decay_up, icl_a0, icl_up, gate_up, rwkv_k_k, rwkv_k_a, rwkv_r_k, ln_x_g, ln_x_b,
              w_out, ffn2_norm, ffn2_gate, ffn2_up, ffn2_down, final_norm)
    y_prompt = _trunk(x_prompt, params)
    y_sample = _trunk(x_sample, params)
    return (y_prompt, y_sample)
```

```python
import functools

import jax
import jax.numpy as jnp
import numpy as np
from jax import lax
from jax.experimental import pallas as pl
from jax.experimental.pallas import tpu as pltpu

F32 = jnp.float32
BF16 = jnp.bfloat16

D_MODEL = 4096
DEPTH = 4
GRID_W = 64
ATT_HEAD_DIM = 128
ATT_WIDTH = D_MODEL // 2
ATT_HEADS = ATT_WIDTH // ATT_HEAD_DIM
ATT_KV_HEADS = ATT_HEADS // 4
ATT_KV_WIDTH = ATT_KV_HEADS * ATT_HEAD_DIM
Q_BLOCK = 128
ROPE_BASE = 10000.0
RWKV_WIDTH = D_MODEL - ATT_WIDTH
RWKV_HEAD_DIM = 64
RWKV_HEADS = RWKV_WIDTH // RWKV_HEAD_DIM
RWKV_CHUNK = 32
DECAY_RANK = 96
ICL_RANK = 96
GATE_RANK = 256
D_FF = 5504
NORM_EPS = 1e-6
GN_EPS = 64e-5
ATT_COLS = ATT_WIDTH + 2 * ATT_KV_WIDTH
RWKV_COLS = 3 * RWKV_WIDTH + 2 * DECAY_RANK + 2 * ICL_RANK + GATE_RANK

LANES = 128
MXU_COLS = 256
VMEM_LIMIT = 56 * 1024 * 1024

D_FF_PAD = -(-D_FF // MXU_COLS) * MXU_COLS
RANK_PAD = LANES
LOWRANK_COLS = 4 * RANK_PAD + GATE_RANK
IN_COLS_PAD = ATT_COLS + 3 * RWKV_WIDTH + LOWRANK_COLS


def _params(sem):
    return pltpu.CompilerParams(dimension_semantics=sem, vmem_limit_bytes=VMEM_LIMIT)


def _rmsnorm_rows(x_ref, g_ref, h_ref, rows):
    bm = x_ref.shape[0]

    def body(i, carry):
        r = pl.multiple_of(i * rows, rows)
        x = x_ref[pl.ds(r, rows), :]
        ms = jnp.mean(x * x, axis=-1, keepdims=True)
        h_ref[pl.ds(r, rows), :] = (x * lax.rsqrt(ms + NORM_EPS) * g_ref[...]).astype(BF16)
        return carry

    lax.fori_loop(0, bm // rows, body, 0)


def _norm_matmul_kernel(x_ref, g_ref, w_ref, o_ref, h_ref):
    @pl.when(pl.program_id(1) == 0)
    def _():
        _rmsnorm_rows(x_ref, g_ref, h_ref, 64)

    o_ref[...] = jnp.dot(h_ref[...], w_ref[...], preferred_element_type=F32).astype(o_ref.dtype)


def _norm_swiglu_kernel(x_ref, g_ref, wg_ref, wu_ref, o_ref, h_ref):
    @pl.when(pl.program_id(1) == 0)
    def _():
        _rmsnorm_rows(x_ref, g_ref, h_ref, 64)

    h = h_ref[...]
    gate = jnp.dot(h, wg_ref[...], preferred_element_type=F32)
    up = jnp.dot(h, wu_ref[...], preferred_element_type=F32)
    o_ref[...] = (gate * jax.nn.sigmoid(gate) * up).astype(o_ref.dtype)


def _matmul_residual_kernel(a_ref, w_ref, r_ref, o_ref, *, scale):
    acc = jnp.dot(a_ref[...], w_ref[...], preferred_element_type=F32)
    o_ref[...] = r_ref[...] + scale * acc


def norm_matmul(x, g, w, *, bm, bn, out_dtype):
    m, k = x.shape
    n = w.shape[1]
    return pl.pallas_call(
        _norm_matmul_kernel,
        out_shape=jax.ShapeDtypeStruct((m, n), out_dtype),
        grid=(m // bm, n // bn),
        in_specs=[pl.BlockSpec((bm, k), lambda i, j: (i, 0)),
                  pl.BlockSpec((1, k), lambda i, j: (0, 0)),
                  pl.BlockSpec((k, bn), lambda i, j: (0, j))],
        out_specs=pl.BlockSpec((bm, bn), lambda i, j: (i, j)),
        scratch_shapes=[pltpu.VMEM((bm, k), BF16)],
        compiler_params=_params(("parallel", "arbitrary")),
        name="norm_matmul",
    )(x, g.reshape(1, k), w)


def norm_swiglu(x, g, wg, wu, *, bm, bn):
    m, k = x.shape
    n = wg.shape[1]
    return pl.pallas_call(
        _norm_swiglu_kernel,
        out_shape=jax.ShapeDtypeStruct((m, n), BF16),
        grid=(m // bm, n // bn),
        in_specs=[pl.BlockSpec((bm, k), lambda i, j: (i, 0)),
                  pl.BlockSpec((1, k), lambda i, j: (0, 0)),
                  pl.BlockSpec((k, bn), lambda i, j: (0, j)),
                  pl.BlockSpec((k, bn), lambda i, j: (0, j))],
        out_specs=pl.BlockSpec((bm, bn), lambda i, j: (i, j)),
        scratch_shapes=[pltpu.VMEM((bm, k), BF16)],
        compiler_params=_params(("parallel", "arbitrary")),
        name="norm_swiglu",
    )(x, g.reshape(1, k), wg, wu)


def matmul_residual(a, w, res, *, scale, bm, bn):
    m, k = a.shape
    n = w.shape[1]
    return pl.pallas_call(
        functools.partial(_matmul_residual_kernel, scale=scale),
        out_shape=jax.ShapeDtypeStruct((m, n), F32),
        grid=(m // bm, n // bn),
        in_specs=[pl.BlockSpec((bm, k), lambda i, j: (i, 0)),
                  pl.BlockSpec((k, bn), lambda i, j: (0, j)),
                  pl.BlockSpec((bm, bn), lambda i, j: (i, j))],
        out_specs=pl.BlockSpec((bm, bn), lambda i, j: (i, j)),
        input_output_aliases={2: 0},
        compiler_params=_params(("parallel", "parallel")),
        name="matmul_residual",
    )(a, w, res)


def _rmsnorm_kernel(x_ref, g_ref, o_ref):
    x = x_ref[...]
    ms = jnp.mean(x * x, axis=-1, keepdims=True)
    o_ref[...] = x * lax.rsqrt(ms + NORM_EPS) * g_ref[...]


def rmsnorm(x, g, *, bm):
    m, k = x.shape
    return pl.pallas_call(
        _rmsnorm_kernel,
        out_shape=jax.ShapeDtypeStruct((m, k), F32),
        grid=(m // bm,),
        in_specs=[pl.BlockSpec((bm, k), lambda i: (i, 0)),
                  pl.BlockSpec((1, k), lambda i: (0, 0))],
        out_specs=pl.BlockSpec((bm, k), lambda i: (i, 0)),
        compiler_params=_params(("parallel",)),
        name="final_rmsnorm",
    )(x, g.reshape(1, k))


def _split_cols(z, sizes):
    idx = [int(i) for i in np.cumsum(sizes)[:-1]]
    return jnp.split(z, idx, axis=-1)


def _rmsnorm_j(x, g):
    xf = x.astype(jnp.float32)
    y = xf * lax.rsqrt(jnp.mean(xf * xf, axis=-1, keepdims=True) + NORM_EPS)
    return (y * g.astype(jnp.float32)).astype(x.dtype)


def _axial_rope_tables(T):
    rows = T // GRID_W
    row = jnp.repeat(jnp.arange(rows), GRID_W).astype(jnp.float32)
    col = jnp.tile(jnp.arange(GRID_W), rows).astype(jnp.float32)
    half = ATT_HEAD_DIM // 2
    inv = jnp.power(ROPE_BASE, -jnp.arange(0, half, 2, dtype=jnp.float32) / half)
    ang_r = row[:, None] * inv[None, :]
    ang_c = col[:, None] * inv[None, :]
    return (jnp.cos(ang_r), jnp.sin(ang_r), jnp.cos(ang_c), jnp.sin(ang_c))


def _rope_half(x, cos, sin):
    x1, x2 = jnp.split(x, 2, axis=-1)
    c = cos[:, None, :]
    s = sin[:, None, :]
    return jnp.concatenate([x1 * c - x2 * s, x1 * s + x2 * c], axis=-1)


def _axial_rope(x, tabs):
    cr, sr, cc, sc = tabs
    xr, xc = jnp.split(x.astype(jnp.float32), 2, axis=-1)
    return jnp.concatenate([_rope_half(xr, cr, sr), _rope_half(xc, cc, sc)], axis=-1)


def _attention_j(q, k, v, q_gain, k_gain, tabs):
    B, T, _ = q.shape
    cd = q.dtype
    G = ATT_HEADS // ATT_KV_HEADS
    scale = ATT_HEAD_DIM ** -0.5
    q = q.reshape(B, T, ATT_HEADS, ATT_HEAD_DIM)
    k = k.reshape(B, T, ATT_KV_HEADS, ATT_HEAD_DIM)
    v = v.reshape(B, T, ATT_KV_HEADS, ATT_HEAD_DIM)
    q = (_axial_rope(_rmsnorm_j(q, q_gain), tabs) * scale).astype(cd)
    k = _axial_rope(_rmsnorm_j(k, k_gain), tabs).astype(cd)
    nblk = T // Q_BLOCK
    qb = q.reshape(B, nblk, Q_BLOCK, ATT_KV_HEADS, G, ATT_HEAD_DIM).transpose(1, 0, 2, 3, 4, 5)

    def block(qi):
        s = jnp.einsum('bqhgd,bkhd->bhgqk', qi, k, preferred_element_type=jnp.float32)
        p = jax.nn.softmax(s, axis=-1).astype(cd)
        return jnp.einsum('bhgqk,bkhd->bqhgd', p, v, preferred_element_type=jnp.float32).astype(cd)

    o = lax.map(block, qb)
    return o.transpose(1, 0, 2, 3, 4, 5).reshape(B, T, ATT_WIDTH)


def _bi_token_shift(z, mu_prev, mu_next):
    zp = jnp.pad(z, ((0, 0), (1, 0), (0, 0)))[:, :-1]
    zn = jnp.pad(z, ((0, 0), (0, 1), (0, 0)))[:, 1:]
    return z + mu_prev * (zp - z) + mu_next * (zn - z)


def _rwkv7_chunked(r, lw, k, v, a, b, inclusive):
    B, T, H, N = r.shape
    L = RWKV_CHUNK
    nC = T // L

    def blk(t):
        return t.reshape(B, nC, L, H, N).transpose(0, 1, 3, 2, 4)

    r, lw, k, v, a, b = (blk(t) for t in (r, lw, k, v, a, b))
    c = jnp.cumsum(lw, axis=3)
    c_ex = c - lw
    m = c[:, :, :, L // 2 - 1:L // 2, :]
    c_end = c[:, :, :, -1:, :]
    c_rd = c if inclusive else c_ex
    a_q = a * jnp.exp(c_ex - m)
    b_k = b * jnp.exp(m - c)
    k_k = k * jnp.exp(m - c)
    r_q = r * jnp.exp(c_rd - m)
    a_0 = a * jnp.exp(c_ex)
    r_0 = r * jnp.exp(c_rd)
    b_e = b * jnp.exp(c_end - c)
    k_e = k * jnp.exp(c_end - c)
    w_e = jnp.exp(c_end[:, :, :, 0, :])
    idx = jnp.arange(L)
    strict = idx[:, None] > idx[None, :]
    rmask = (idx[:, None] >= idx[None, :]) if inclusive else strict
    A = jnp.where(strict, jnp.einsum('bchtk,bchsk->bchts', a_q, b_k), 0.0)
    Bm = jnp.where(strict, jnp.einsum('bchtk,bchsk->bchts', a_q, k_k), 0.0)
    Mb = jnp.where(rmask, jnp.einsum('bchtk,bchsk->bchts', r_q, b_k), 0.0)
    Mk = jnp.where(rmask, jnp.einsum('bchtk,bchsk->bchts', r_q, k_k), 0.0)
    eye = jnp.eye(L, dtype=jnp.float32)
    inv = lax.linalg.triangular_solve(eye - A, jnp.broadcast_to(eye, A.shape), left_side=True, lower=True)
    P = jnp.einsum('bchts,bchsk->bchtk', inv, a_0)
    Q = jnp.einsum('bchts,bchsv->bchtv', inv, jnp.einsum('bchts,bchsv->bchtv', Bm, v))
    Yl = jnp.einsum('bchts,bchsv->bchtv', Mk, v)
    xs = tuple(jnp.moveaxis(t, 1, 0) for t in (P, Q, r_0, Mb, Yl, b_e, k_e, v, w_e))

    def step(S, inp):
        P_c, Q_c, r0_c, Mb_c, Yl_c, be_c, ke_c, v_c, we_c = inp
        U = jnp.einsum('bhlk,bhvk->bhlv', P_c, S) + Q_c
        Y = jnp.einsum('bhlk,bhvk->bhlv', r0_c, S) + jnp.einsum('bhls,bhsv->bhlv', Mb_c, U) + Yl_c
        S_new = (S * we_c[:, :, None, :] + jnp.einsum('bhsv,bhsk->bhvk', U, be_c)
                 + jnp.einsum('bhsv,bhsk->bhvk', v_c, ke_c))
        return S_new, Y

    S0 = jnp.zeros((B, H, N, N), jnp.float32)
    _, ys = lax.scan(step, S0, xs)
    return ys.transpose(1, 0, 3, 2, 4).reshape(B, T, H, N)


def _rwkv7_mixer_j(z, w0, w_up, a0, a_up, g_up, k_k, k_a, r_k, ln_g, ln_b):
    B, T, _ = z.shape
    f32 = jnp.float32
    r, k, v, wd_f, wd_b, ad_f, ad_b, gd = _split_cols(
        z.astype(f32), (RWKV_WIDTH, RWKV_WIDTH, RWKV_WIDTH, DECAY_RANK, DECAY_RANK, ICL_RANK, ICL_RANK, GATE_RANK))

    def heads(t):
        return t.reshape(B, T, RWKV_HEADS, RWKV_HEAD_DIM)

    kk = heads(k * k_k)
    kk = kk * lax.rsqrt(jnp.maximum(jnp.sum(kk * kk, axis=-1, keepdims=True), 1e-24))
    rh = heads(r)
    vh = heads(v)

    def direction(wd, w0_d, w_up_d, ad, a0_d, a_up_d, backward):
        w_log = -jax.nn.softplus(-(w0_d + jnp.tanh(wd) @ w_up_d)) - 0.5
        lw = -jnp.exp(w_log)
        a = jax.nn.sigmoid(a0_d + ad @ a_up_d)
        k_d = k * (1.0 + (a - 1.0) * k_a)
        args = (rh, heads(lw), heads(k_d), vh, -kk, kk * heads(a))
        if backward:
            args = tuple(t[:, ::-1] for t in args)
            y = _rwkv7_chunked(*args, inclusive=False)[:, ::-1]
        else:
            y = _rwkv7_chunked(*args, inclusive=True)
        return y, k_d

    y_f, k_f = direction(wd_f, w0[0], w_up[0], ad_f, a0[0], a_up[0], False)
    y_b, _ = direction(wd_b, w0[1], w_up[1], ad_b, a0[1], a_up[1], True)
    y = y_f + y_b
    mu = jnp.mean(y, axis=-1, keepdims=True)
    var = jnp.mean(jnp.square(y - mu), axis=-1, keepdims=True)
    yn = ((y - mu) * lax.rsqrt(var + GN_EPS)).reshape(B, T, RWKV_WIDTH) * ln_g + ln_b
    bonus = jnp.sum(rh * heads(k_f) * r_k, axis=-1, keepdims=True) * vh
    g = jax.nn.sigmoid(gd) @ g_up
    return ((yn + bonus.reshape(B, T, RWKV_WIDTH)) * g).astype(z.dtype)


def _pad_cols(w, n):
    return jnp.pad(w, [(0, 0)] * (w.ndim - 1) + [(0, n - w.shape[-1])])


def _pack_w_in(w_in):
    main = ATT_COLS + 3 * RWKV_WIDTH
    parts = [w_in[..., :main]]
    off = main
    for _ in range(4):
        parts.append(_pad_cols(w_in[..., off:off + DECAY_RANK], RANK_PAD))
        off += DECAY_RANK
    parts.append(w_in[..., off:])
    return jnp.concatenate(parts, axis=-1).astype(BF16)


def _trunk(x3, wts, lp):
    B, T, D = x3.shape
    x = x3.reshape(B * T, D)
    tabs = _axial_rope_tables(T)
    main = ATT_COLS + 3 * RWKV_WIDTH
    for l in range(DEPTH):
        act = norm_swiglu(x, lp['ffn1_norm'][l], wts['g1'][l], wts['u1'][l], bm=512, bn=512)
        x = matmul_residual(act, wts['d1'][l], x, scale=0.5, bm=512, bn=1024)
        zin = norm_matmul(x, lp['mix_norm'][l], wts['w_in'][l], bm=512, bn=768, out_dtype=F32)
        zin = zin.reshape(B, T, IN_COLS_PAD)
        q = zin[..., :ATT_WIDTH]
        k = zin[..., ATT_WIDTH:ATT_WIDTH + ATT_KV_WIDTH]
        v = zin[..., ATT_WIDTH + ATT_KV_WIDTH:ATT_COLS]
        att_out = _attention_j(q, k, v, lp['q_norm'][l], lp['k_norm'][l], tabs)
        lr = [zin[..., main + i * RANK_PAD: main + i * RANK_PAD + DECAY_RANK] for i in range(4)]
        z_rwkv = jnp.concatenate([zin[..., ATT_COLS:main]] + lr + [zin[..., main + 4 * RANK_PAD:]], axis=-1)
        z_rwkv = _bi_token_shift(z_rwkv, lp['shift_mu'][l, 0], lp['shift_mu'][l, 1])
        rwkv_out = _rwkv7_mixer_j(z_rwkv, lp['decay_w0'][l], lp['decay_up'][l], lp['icl_a0'][l], lp['icl_up'][l],
                                  lp['gate_up'][l], lp['rwkv_k_k'][l], lp['rwkv_k_a'][l], lp['rwkv_r_k'][l],
                                  lp['ln_x_g'][l], lp['ln_x_b'][l])
        mix = jnp.concatenate([att_out, rwkv_out], axis=-1).reshape(B * T, D).astype(BF16)
        x = matmul_residual(mix, wts['w_out'][l], x, scale=1.0, bm=512, bn=1024)
        act = norm_swiglu(x, lp['ffn2_norm'][l], wts['g2'][l], wts['u2'][l], bm=512, bn=512)
        x = matmul_residual(act, wts['d2'][l], x, scale=0.5, bm=512, bn=1024)
    return rmsnorm(x, lp['final_norm'], bm=256).reshape(B, T, D)


def kernel(x_prompt, x_sample, ffn1_norm, ffn1_gate, ffn1_up, ffn1_down, mix_norm, w_in, q_norm, k_norm, shift_mu, decay_w0, decay_up, icl_a0, icl_up, gate_up, rwkv_k_k, rwkv_k_a, rwkv_r_k, ln_x_g, ln_x_b, w_out, ffn2_norm, ffn2_gate, ffn2_up, ffn2_down, final_norm):
    lp = dict(ffn1_norm=ffn1_norm, mix_norm=mix_norm, q_norm=q_norm, k_norm=k_norm, shift_mu=shift_mu,
              decay_w0=decay_w0, decay_up=decay_up, icl_a0=icl_a0, icl_up=icl_up, gate_up=gate_up,
              rwkv_k_k=rwkv_k_k, rwkv_k_a=rwkv_k_a, rwkv_r_k=rwkv_r_k, ln_x_g=ln_x_g, ln_x_b=ln_x_b,
              ffn2_norm=ffn2_norm, final_norm=final_norm)
    wts = dict(
        g1=_pad_cols(ffn1_gate, D_FF_PAD).astype(BF16),
        u1=_pad_cols(ffn1_up, D_FF_PAD).astype(BF16),
        d1=jnp.pad(ffn1_down, ((0, 0), (0, D_FF_PAD - D_FF), (0, 0))).astype(BF16),
        g2=_pad_cols(ffn2_gate, D_FF_PAD).astype(BF16),
        u2=_pad_cols(ffn2_up, D_FF_PAD).astype(BF16),
        d2=jnp.pad(ffn2_down, ((0, 0), (0, D_FF_PAD - D_FF), (0, 0))).astype(BF16),
        w_in=_pack_w_in(w_in),
        w_out=w_out.astype(BF16),
    )
    return (_trunk(x_prompt, wts, lp), _trunk(x_sample, wts, lp))
```

```python
import functools

import jax
import jax.numpy as jnp
import numpy as np
from jax import lax
from jax.experimental import pallas as pl
from jax.experimental.pallas import tpu as pltpu

F32 = jnp.float32
BF16 = jnp.bfloat16

D_MODEL = 4096
DEPTH = 4
GRID_W = 64
ATT_HEAD_DIM = 128
ATT_WIDTH = D_MODEL // 2
ATT_HEADS = ATT_WIDTH // ATT_HEAD_DIM
ATT_KV_HEADS = ATT_HEADS // 4
ATT_KV_WIDTH = ATT_KV_HEADS * ATT_HEAD_DIM
ROPE_BASE = 10000.0
RWKV_WIDTH = D_MODEL - ATT_WIDTH
RWKV_HEAD_DIM = 64
RWKV_HEADS = RWKV_WIDTH // RWKV_HEAD_DIM
DECAY_RANK = 96
ICL_RANK = 96
GATE_RANK = 256
D_FF = 5504
NORM_EPS = 1e-6
GN_EPS = 64e-5
ATT_COLS = ATT_WIDTH + 2 * ATT_KV_WIDTH

LANES = 128
MXU_COLS = 256
VMEM_LIMIT = 56 * 1024 * 1024

D_FF_PAD = -(-D_FF // MXU_COLS) * MXU_COLS
RANK_PAD = LANES
LOWRANK_COLS = 4 * RANK_PAD + GATE_RANK
IN_COLS_PAD = ATT_COLS + 3 * RWKV_WIDTH + LOWRANK_COLS


def _params(sem):
    return pltpu.CompilerParams(dimension_semantics=sem, vmem_limit_bytes=VMEM_LIMIT)


def _rmsnorm_rows(x_ref, g_ref, h_ref, rows):
    bm = x_ref.shape[0]

    def body(i, carry):
        r = pl.multiple_of(i * rows, rows)
        x = x_ref[pl.ds(r, rows), :]
        ms = jnp.mean(x * x, axis=-1, keepdims=True)
        h_ref[pl.ds(r, rows), :] = (x * lax.rsqrt(ms + NORM_EPS) * g_ref[...]).astype(BF16)
        return carry

    lax.fori_loop(0, bm // rows, body, 0)


def _norm_matmul_kernel(x_ref, g_ref, w_ref, o_ref, h_ref):
    @pl.when(pl.program_id(1) == 0)
    def _():
        _rmsnorm_rows(x_ref, g_ref, h_ref, 64)

    o_ref[...] = jnp.dot(h_ref[...], w_ref[...], preferred_element_type=F32).astype(o_ref.dtype)


def _norm_swiglu_kernel(x_ref, g_ref, wg_ref, wu_ref, o_ref, h_ref):
    @pl.when(pl.program_id(1) == 0)
    def _():
        _rmsnorm_rows(x_ref, g_ref, h_ref, 64)

    h = h_ref[...]
    gate = jnp.dot(h, wg_ref[...], preferred_element_type=F32)
    up = jnp.dot(h, wu_ref[...], preferred_element_type=F32)
    o_ref[...] = (gate * jax.nn.sigmoid(gate) * up).astype(o_ref.dtype)


def _matmul_residual_kernel(a_ref, w_ref, r_ref, o_ref, *, scale):
    acc = jnp.dot(a_ref[...], w_ref[...], preferred_element_type=F32)
    o_ref[...] = r_ref[...] + scale * acc


def norm_matmul(x, g, w, *, bm, bn, out_dtype):
    m, k = x.shape
    n = w.shape[1]
    return pl.pallas_call(
        _norm_matmul_kernel,
        out_shape=jax.ShapeDtypeStruct((m, n), out_dtype),
        grid=(m // bm, n // bn),
        in_specs=[pl.BlockSpec((bm, k), lambda i, j: (i, 0)),
                  pl.BlockSpec((1, k), lambda i, j: (0, 0)),
                  pl.BlockSpec((k, bn), lambda i, j: (0, j))],
        out_specs=pl.BlockSpec((bm, bn), lambda i, j: (i, j)),
        scratch_shapes=[pltpu.VMEM((bm, k), BF16)],
        compiler_params=_params(("parallel", "arbitrary")),
        name="norm_matmul",
    )(x, g.reshape(1, k), w)


def norm_swiglu(x, g, wg, wu, *, bm, bn):
    m, k = x.shape
    n = wg.shape[1]
    return pl.pallas_call(
        _norm_swiglu_kernel,
        out_shape=jax.ShapeDtypeStruct((m, n), BF16),
        grid=(m // bm, n // bn),
        in_specs=[pl.BlockSpec((bm, k), lambda i, j: (i, 0)),
                  pl.BlockSpec((1, k), lambda i, j: (0, 0)),
                  pl.BlockSpec((k, bn), lambda i, j: (0, j)),
                  pl.BlockSpec((k, bn), lambda i, j: (0, j))],
        out_specs=pl.BlockSpec((bm, bn), lambda i, j: (i, j)),
        scratch_shapes=[pltpu.VMEM((bm, k), BF16)],
        compiler_params=_params(("parallel", "arbitrary")),
        name="norm_swiglu",
    )(x, g.reshape(1, k), wg, wu)


def matmul_residual(a, w, res, *, scale, bm, bn):
    m, k = a.shape
    n = w.shape[1]
    return pl.pallas_call(
        functools.partial(_matmul_residual_kernel, scale=scale),
        out_shape=jax.ShapeDtypeStruct((m, n), F32),
        grid=(m // bm, n // bn),
        in_specs=[pl.BlockSpec((bm, k), lambda i, j: (i, 0)),
                  pl.BlockSpec((k, bn), lambda i, j: (0, j)),
                  pl.BlockSpec((bm, bn), lambda i, j: (i, j))],
        out_specs=pl.BlockSpec((bm, bn), lambda i, j: (i, j)),
        input_output_aliases={2: 0},
        compiler_params=_params(("parallel", "parallel")),
        name="matmul_residual",
    )(a, w, res)


def _matmul2_residual_kernel(a1_ref, a2_ref, w_ref, r_ref, o_ref):
    k1 = a1_ref.shape[1]
    acc = jnp.dot(a1_ref[...], w_ref[:k1, :], preferred_element_type=F32)
    acc = acc + jnp.dot(a2_ref[...], w_ref[k1:, :], preferred_element_type=F32)
    o_ref[...] = r_ref[...] + acc


def matmul2_residual(a1, a2, w, res, *, bm, bn):
    m, k1 = a1.shape
    k2 = a2.shape[1]
    n = w.shape[1]
    return pl.pallas_call(
        _matmul2_residual_kernel,
        out_shape=jax.ShapeDtypeStruct((m, n), F32),
        grid=(m // bm, n // bn),
        in_specs=[pl.BlockSpec((bm, k1), lambda i, j: (i, 0)),
                  pl.BlockSpec((bm, k2), lambda i, j: (i, 0)),
                  pl.BlockSpec((k1 + k2, bn), lambda i, j: (0, j)),
                  pl.BlockSpec((bm, bn), lambda i, j: (i, j))],
        out_specs=pl.BlockSpec((bm, bn), lambda i, j: (i, j)),
        input_output_aliases={3: 0},
        compiler_params=_params(("parallel", "parallel")),
        name="matmul2_residual",
    )(a1, a2, w, res)


def _rmsnorm_kernel(x_ref, g_ref, o_ref):
    x = x_ref[...]
    ms = jnp.mean(x * x, axis=-1, keepdims=True)
    o_ref[...] = x * lax.rsqrt(ms + NORM_EPS) * g_ref[...]


def rmsnorm(x, g, *, bm):
    m, k = x.shape
    return pl.pallas_call(
        _rmsnorm_kernel,
        out_shape=jax.ShapeDtypeStruct((m, k), F32),
        grid=(m // bm,),
        in_specs=[pl.BlockSpec((bm, k), lambda i: (i, 0)),
                  pl.BlockSpec((1, k), lambda i: (0, 0))],
        out_specs=pl.BlockSpec((bm, k), lambda i: (i, 0)),
        compiler_params=_params(("parallel",)),
        name="final_rmsnorm",
    )(x, g.reshape(1, k))


def _rope_tables(T):
    half = ATT_HEAD_DIM // 2
    quarter = half // 2
    t = jnp.arange(T)
    pos = jnp.stack([(t // GRID_W), (t % GRID_W)], axis=1).astype(F32)
    inv = jnp.power(ROPE_BASE, -jnp.arange(0, half, 2, dtype=F32) / half)
    lane = np.arange(ATT_HEAD_DIM)
    ang = pos[:, lane // half] * inv[lane % quarter][None, :]
    first = jnp.asarray((lane % half) < quarter)[None, :]
    cos, sin = jnp.cos(ang), jnp.sin(ang)
    return cos, jnp.where(first, -sin, 0.0), jnp.where(first, 0.0, sin)


def _qk_prep_kernel(q_ref, k_ref, v_ref, c_ref, sa_ref, sb_ref, qg_ref, kg_ref, qo_ref, ko_ref, vo_ref):
    c, sa, sb = c_ref[...], sa_ref[...], sb_ref[...]
    quarter = ATT_HEAD_DIM // 4

    def prep(x, gain, scale):
        ms = jnp.mean(x * x, axis=-1, keepdims=True)
        y = x * lax.rsqrt(ms + NORM_EPS) * gain
        y = y * c + pltpu.roll(y, ATT_HEAD_DIM - quarter, 1) * sa + pltpu.roll(y, quarter, 1) * sb
        return (y * scale).astype(BF16)

    for h in range(ATT_HEADS):
        sl = slice(h * ATT_HEAD_DIM, (h + 1) * ATT_HEAD_DIM)
        qo_ref[:, sl] = prep(q_ref[:, sl], qg_ref[...], ATT_HEAD_DIM ** -0.5)
    for h in range(ATT_KV_HEADS):
        sl = slice(h * ATT_HEAD_DIM, (h + 1) * ATT_HEAD_DIM)
        ko_ref[:, sl] = prep(k_ref[:, sl], kg_ref[...], 1.0)
    vo_ref[...] = v_ref[...].astype(BF16)


def qk_prep(zin, tabs, q_gain, k_gain, *, T, tm):
    n = zin.shape[0]
    tps = T // tm
    kvb = ATT_WIDTH // ATT_KV_WIDTH
    tab_spec = pl.BlockSpec((tm, ATT_HEAD_DIM), lambda i: (i % tps, 0))
    vec_spec = pl.BlockSpec((1, ATT_HEAD_DIM), lambda i: (0, 0))
    return pl.pallas_call(
        _qk_prep_kernel,
        out_shape=(jax.ShapeDtypeStruct((n, ATT_WIDTH), BF16),
                   jax.ShapeDtypeStruct((n, ATT_KV_WIDTH), BF16),
                   jax.ShapeDtypeStruct((n, ATT_KV_WIDTH), BF16)),
        grid=(n // tm,),
        in_specs=[pl.BlockSpec((tm, ATT_WIDTH), lambda i: (i, 0)),
                  pl.BlockSpec((tm, ATT_KV_WIDTH), lambda i: (i, kvb)),
                  pl.BlockSpec((tm, ATT_KV_WIDTH), lambda i: (i, kvb + 1)),
                  tab_spec, tab_spec, tab_spec, vec_spec, vec_spec],
        out_specs=(pl.BlockSpec((tm, ATT_WIDTH), lambda i: (i, 0)),
                   pl.BlockSpec((tm, ATT_KV_WIDTH), lambda i: (i, 0)),
                   pl.BlockSpec((tm, ATT_KV_WIDTH), lambda i: (i, 0))),
        compiler_params=_params(("parallel",)),
        name="qk_prep",
    )(zin, zin, zin, *tabs, q_gain.reshape(1, -1), k_gain.reshape(1, -1))


def _attn_kernel(q_ref, k_ref, v_ref, o_ref, m_ref, l_ref, acc_ref, *, kc):
    bq = q_ref.shape[0]
    T = k_ref.shape[0]
    G = ATT_HEADS // ATT_KV_HEADS
    hd = ATT_HEAD_DIM
    q = jnp.concatenate([q_ref[:, g * hd:(g + 1) * hd] for g in range(G)], axis=0)
    m_ref[...] = jnp.full(m_ref.shape, -jnp.inf, F32)
    l_ref[...] = jnp.zeros(l_ref.shape, F32)
    acc_ref[...] = jnp.zeros(acc_ref.shape, F32)

    def body(c, carry):
        r0 = pl.multiple_of(c * kc, kc)
        kch = k_ref[pl.ds(r0, kc), :]
        vch = v_ref[pl.ds(r0, kc), :]
        s = lax.dot_general(q, kch, (((1,), (1,)), ((), ())), preferred_element_type=F32)
        m_prev = m_ref[...]
        m_new = jnp.maximum(m_prev, jnp.max(s, axis=-1, keepdims=True))
        alpha = jnp.exp(m_prev - m_new)
        p = jnp.exp(s - m_new)
        l_ref[...] = alpha * l_ref[...] + jnp.sum(p, axis=-1, keepdims=True)
        acc_ref[...] = alpha * acc_ref[...] + jnp.dot(p.astype(BF16), vch, preferred_element_type=F32)
        m_ref[...] = m_new
        return carry

    lax.fori_loop(0, T // kc, body, 0)
    o = acc_ref[...] / l_ref[...]
    for g in range(G):
        o_ref[:, g * hd:(g + 1) * hd] = o[g * bq:(g + 1) * bq].astype(o_ref.dtype)


def attention(q, k, v, *, T, bq, kc):
    n = q.shape[0]
    n_seq = n // T
    G = ATT_HEADS // ATT_KV_HEADS
    gw = G * ATT_HEAD_DIM
    nq = T // bq
    return pl.pallas_call(
        functools.partial(_attn_kernel, kc=kc),
        out_shape=jax.ShapeDtypeStruct((n, ATT_WIDTH), BF16),
        grid=(n_seq, ATT_KV_HEADS, nq),
        in_specs=[pl.BlockSpec((bq, gw), lambda s, h, i: (s * nq + i, h)),
                  pl.BlockSpec((T, ATT_HEAD_DIM), lambda s, h, i: (s, h)),
                  pl.BlockSpec((T, ATT_HEAD_DIM), lambda s, h, i: (s, h))],
        out_specs=pl.BlockSpec((bq, gw), lambda s, h, i: (s * nq + i, h)),
        scratch_shapes=[pltpu.VMEM((G * bq, 1), F32), pltpu.VMEM((G * bq, 1), F32),
                        pltpu.VMEM((G * bq, ATT_HEAD_DIM), F32)],
        compiler_params=_params(("parallel", "parallel", "arbitrary")),
        name="attention",
    )(q, k, v)


PAIR = 2 * RWKV_HEAD_DIM
N_PAIRS = RWKV_WIDTH // PAIR
CHUNK = 64
HI = lax.Precision.HIGHEST


def _seg_matrix():
    lane = np.arange(PAIR)
    return jnp.asarray((lane[:, None] // RWKV_HEAD_DIM == lane[None, :] // RWKV_HEAD_DIM).astype(np.float32))


def _softplus(x):
    return jnp.maximum(x, 0.0) + jnp.log(1.0 + jnp.exp(-jnp.abs(x)))


def _rwkv_prep_kernel(zr_ref, zrp_ref, zrn_ref, zk_ref, zkp_ref, zkn_ref, zv_ref, zvp_ref, zvn_ref,
                      zl_ref, zlp_ref, zln_ref, mur_ref, muk_ref, muv_ref, mul_ref,
                      wup_ref, aup_ref, gup_ref, vec_ref, seg_ref,
                      r_o, v_o, kk_o, lwf_o, lwb_o, kdf_o, kdb_o, bf_o, bb_o, g_o, bg_o,
                      lr_ref, *, tiles_per_seq):
    i = pl.program_id(0)
    tm = zr_ref.shape[0]
    first = (i % tiles_per_seq) == 0
    last = (i % tiles_per_seq) == tiles_per_seq - 1
    rows = lax.broadcasted_iota(jnp.int32, (tm, 1), 0)

    def shift(z_ref, zp_ref, zn_ref, mu_ref):
        z = z_ref[...]
        prev_row = jnp.where(first, 0.0, zp_ref[7:8, :])
        next_row = jnp.where(last, 0.0, zn_ref[0:1, :])
        zp = jnp.where(rows == 0, prev_row, pltpu.roll(z, 1, 0))
        zn = jnp.where(rows == tm - 1, next_row, pltpu.roll(z, tm - 1, 0))
        return z + mu_ref[0:1, :] * (zp - z) + mu_ref[1:2, :] * (zn - z)

    @pl.when(pl.program_id(1) == 0)
    def _():
        lr = shift(zl_ref, zlp_ref, zln_ref, mul_ref)
        nd = 2 * RANK_PAD
        lr_ref[:, :nd] = jnp.tanh(lr[:, :nd]).astype(BF16)
        lr_ref[:, nd:2 * nd] = lr[:, nd:2 * nd].astype(BF16)
        lr_ref[:, 2 * nd:] = jax.nn.sigmoid(lr[:, 2 * nd:]).astype(BF16)

    r = shift(zr_ref, zrp_ref, zrn_ref, mur_ref)
    k = shift(zk_ref, zkp_ref, zkn_ref, muk_ref)
    v = shift(zv_ref, zvp_ref, zvn_ref, muv_ref)
    seg = seg_ref[...]
    k_k, k_a, r_k = vec_ref[4:5, :], vec_ref[5:6, :], vec_ref[6:7, :]

    kk = k * k_k
    ss = jnp.dot(kk * kk, seg, preferred_element_type=F32, precision=HI)
    kk = kk * lax.rsqrt(jnp.maximum(ss, 1e-24))

    def direction(d):
        tw = lr_ref[:, d * RANK_PAD:(d + 1) * RANK_PAD]
        ad = lr_ref[:, (2 + d) * RANK_PAD:(3 + d) * RANK_PAD]
        u = vec_ref[d:d + 1, :] + jnp.dot(tw, wup_ref[d], preferred_element_type=F32)
        w_log = -_softplus(-u) - 0.5
        lw = -jnp.exp(w_log)
        a = jax.nn.sigmoid(vec_ref[2 + d:3 + d, :] + jnp.dot(ad, aup_ref[d], preferred_element_type=F32))
        kd = k * (1.0 + (a - 1.0) * k_a)
        return lw, kd, kk * a

    lw_f, kd_f, b_f = direction(0)
    lw_b, kd_b, b_b = direction(1)
    g = jnp.dot(lr_ref[:, 4 * RANK_PAD:], gup_ref[...], preferred_element_type=F32)
    bonus = jnp.dot(r * kd_f * r_k, seg, preferred_element_type=F32, precision=HI) * v
    r_o[...] = r
    v_o[...] = v
    kk_o[...] = kk
    lwf_o[...] = lw_f
    lwb_o[...] = lw_b
    kdf_o[...] = kd_f
    kdb_o[...] = kd_b
    bf_o[...] = b_f
    bb_o[...] = b_b
    g_o[...] = g
    bg_o[...] = bonus * g


def rwkv_prep(zin, mu, wup, aup, gup, vecs, *, T, tm):
    n = zin.shape[0]
    tps = T // tm
    hb = tm // 8
    nb8 = n // 8
    c_r = ATT_COLS // PAIR
    c_k = c_r + N_PAIRS
    c_v = c_k + N_PAIRS
    c_l = (ATT_COLS + 3 * RWKV_WIDTH) // LOWRANK_COLS
    assert c_l * LOWRANK_COLS == ATT_COLS + 3 * RWKV_WIDTH

    def main_specs(c0):
        return [pl.BlockSpec((tm, PAIR), lambda i, p: (i, c0 + p)),
                pl.BlockSpec((8, PAIR), lambda i, p: (jnp.maximum(i * hb - 1, 0), c0 + p)),
                pl.BlockSpec((8, PAIR), lambda i, p: (jnp.minimum((i + 1) * hb, nb8 - 1), c0 + p))]

    low_specs = [pl.BlockSpec((tm, LOWRANK_COLS), lambda i, p: (i, c_l)),
                 pl.BlockSpec((8, LOWRANK_COLS), lambda i, p: (jnp.maximum(i * hb - 1, 0), c_l)),
                 pl.BlockSpec((8, LOWRANK_COLS), lambda i, p: (jnp.minimum((i + 1) * hb, nb8 - 1), c_l))]
    mu_specs = [pl.BlockSpec((2, PAIR), lambda i, p: (0, p)),
                pl.BlockSpec((2, PAIR), lambda i, p: (0, N_PAIRS + p)),
                pl.BlockSpec((2, PAIR), lambda i, p: (0, 2 * N_PAIRS + p)),
                pl.BlockSpec((2, LOWRANK_COLS), lambda i, p: (0, 3 * RWKV_WIDTH // LOWRANK_COLS))]
    w_specs = [pl.BlockSpec((2, RANK_PAD, PAIR), lambda i, p: (0, 0, p)),
               pl.BlockSpec((2, RANK_PAD, PAIR), lambda i, p: (0, 0, p)),
               pl.BlockSpec((GATE_RANK, PAIR), lambda i, p: (0, p)),
               pl.BlockSpec((8, PAIR), lambda i, p: (0, p)),
               pl.BlockSpec((PAIR, PAIR), lambda i, p: (0, 0))]
    out_spec = pl.BlockSpec((tm, PAIR), lambda i, p: (i, p))
    n_out = 11
    return pl.pallas_call(
        functools.partial(_rwkv_prep_kernel, tiles_per_seq=tps),
        out_shape=tuple(jax.ShapeDtypeStruct((n, RWKV_WIDTH), F32) for _ in range(n_out)),
        grid=(n // tm, N_PAIRS),
        in_specs=main_specs(c_r) + main_specs(c_k) + main_specs(c_v) + low_specs + mu_specs + w_specs,
        out_specs=tuple(out_spec for _ in range(n_out)),
        scratch_shapes=[pltpu.VMEM((tm, LOWRANK_COLS), BF16)],
        compiler_params=_params(("parallel", "arbitrary")),
        name="rwkv_prep",
    )(*([zin] * 12), mu, mu, mu, mu, wup, aup, gup, vecs, _seg_matrix())


def _chunk_masks(bwd):
    idx = np.arange(PAIR)
    h, t = idx // CHUNK, idx % CHUNK
    same = h[:, None] == h[None, :]
    tt, ss = (t[None, :], t[:, None]) if bwd else (t[:, None], t[None, :])
    strict = same & (tt > ss)
    read = strict if bwd else same & (tt >= ss)

    def bd(w):
        return same & (t[:, None] // w == t[None, :] // w)

    lv = [bd(8), bd(16) & ~bd(8), bd(32) & ~bd(16), same & ~bd(32)]
    m = np.stack([strict, read] + lv + [np.eye(PAIR, dtype=bool)]).astype(np.float32)
    tc = np.arange(CHUNK)
    cum = (tc[:, None] <= tc[None, :]) if bwd else (tc[:, None] >= tc[None, :])
    return jnp.asarray(m), jnp.asarray(cum.astype(np.float32))


def _bdot(a, b):
    return jnp.dot(a.astype(BF16), b.astype(BF16), preferred_element_type=F32)


def _bdot_nt(a, b):
    return lax.dot_general(a.astype(BF16), b.astype(BF16), (((1,), (1,)), ((), ())), preferred_element_type=F32)


def _bdot_tn(a, b):
    return lax.dot_general(a.astype(BF16), b.astype(BF16), (((0,), (0,)), ((), ())), preferred_element_type=F32)


def _rwkv_dir_kernel(r_ref, lw_ref, kd_ref, v_ref, kk_ref, b_ref, mask_ref, cum_ref, y_ref, s_ref, *, bwd):
    n_chunks = r_ref.shape[0] // CHUNK
    P = PAIR

    @pl.when(pl.program_id(2) == 0)
    def _():
        s_ref[...] = jnp.zeros(s_ref.shape, F32)

    head_a = lax.broadcasted_iota(jnp.int32, (1, P), 1) < RWKV_HEAD_DIM
    mid = CHUNK // 2 if bwd else CHUNK // 2 - 1
    end = 0 if bwd else CHUNK - 1

    def stack(x):
        return jnp.concatenate([jnp.where(head_a, x, 0.0), jnp.where(head_a, 0.0, x)], axis=0)

    def chunk(ci, carry):
        c = (n_chunks - 1 - ci) if bwd else ci
        sl = pl.ds(pl.multiple_of(c * CHUNK, CHUNK), CHUNK)
        r, lw, kd, v, kk, b = r_ref[sl, :], lw_ref[sl, :], kd_ref[sl, :], v_ref[sl, :], kk_ref[sl, :], b_ref[sl, :]
        strict, read, eye = mask_ref[0], mask_ref[1], mask_ref[6]

        cum = jnp.dot(cum_ref[...], lw, preferred_element_type=F32, precision=HI)
        cex = cum - lw
        m = cum[mid:mid + 1, :]
        cend = cum[end:end + 1, :]
        e1 = jnp.exp(cex - m)
        e2 = jnp.exp(m - cum)
        e3 = e1 if bwd else jnp.exp(cum - m)
        em = jnp.exp(m)
        eem = jnp.exp(cend - m)
        we = jnp.exp(cend)
        a_q = -kk * e1
        b_k = b * e2
        k_k = kd * e2
        r_q = r * e3

        sc = _bdot_nt(jnp.concatenate([stack(a_q), stack(r_q)], axis=0),
                      jnp.concatenate([stack(b_k), stack(k_k)], axis=0))
        A = sc[:P, :P] * strict
        Bm = sc[:P, P:] * strict
        Mb = sc[P:, :P] * read
        Mk = sc[P:, P:] * read

        D = A * mask_ref[2]
        D2 = _bdot(D, D)
        X = eye + D + D2 + _bdot(D2, D)
        X = X + _bdot(_bdot(D2, D2), X)
        for lvl in (3, 4, 5):
            X = X + _bdot(_bdot(X, A * mask_ref[lvl]), X)

        vst = stack(v)
        bv = _bdot(jnp.concatenate([Bm, Mk], axis=0), vst)
        pq = _bdot(X, jnp.concatenate([stack(a_q * em), bv[:P]], axis=1))
        S = s_ref[...]
        ps = _bdot_nt(jnp.concatenate([pq[:, :P], stack(r_q * em)], axis=0), S)
        U = ps[:P] + pq[:, P:]
        Y = ps[P:] + _bdot(Mb, U) + bv[P:]
        y_ref[sl, :] = Y[:CHUNK] + Y[CHUNK:]
        s_ref[...] = S * we + _bdot_tn(jnp.concatenate([U, vst], axis=0),
                                       jnp.concatenate([stack(b_k * eem), stack(k_k * eem)], axis=0))
        return carry

    lax.fori_loop(0, n_chunks, chunk, 0)


def rwkv_dir(r, lw, kd, v, kk, b, *, T, tt, bwd):
    n = r.shape[0]
    n_seq = n // T
    nt = T // tt
    masks, cum = _chunk_masks(bwd)
    if bwd:
        spec = pl.BlockSpec((tt, PAIR), lambda s, p, j: (s * nt + nt - 1 - j, p))
    else:
        spec = pl.BlockSpec((tt, PAIR), lambda s, p, j: (s * nt + j, p))
    return pl.pallas_call(
        functools.partial(_rwkv_dir_kernel, bwd=bwd),
        out_shape=jax.ShapeDtypeStruct((n, RWKV_WIDTH), F32),
        grid=(n_seq, N_PAIRS, nt),
        in_specs=[spec] * 6 + [pl.BlockSpec(masks.shape, lambda s, p, j: (0, 0, 0)),
                               pl.BlockSpec(cum.shape, lambda s, p, j: (0, 0))],
        out_specs=spec,
        scratch_shapes=[pltpu.VMEM((PAIR, PAIR), F32)],
        compiler_params=_params(("parallel", "parallel", "arbitrary")),
        name="rwkv_bwd" if bwd else "rwkv_fwd",
    )(r, lw, kd, v, kk, b, masks, cum)


def _rwkv_post_kernel(yf_ref, yb_ref, g_ref, bg_ref, ln_ref, seg_ref, o_ref):
    y = yf_ref[...] + yb_ref[...]
    segm = seg_ref[...] * (1.0 / RWKV_HEAD_DIM)
    mu = jnp.dot(y, segm, preferred_element_type=F32, precision=HI)
    d = y - mu
    var = jnp.dot(d * d, segm, preferred_element_type=F32, precision=HI)
    yn = d * lax.rsqrt(var + GN_EPS) * ln_ref[0:1, :] + ln_ref[1:2, :]
    o_ref[...] = (yn * g_ref[...] + bg_ref[...]).astype(o_ref.dtype)


def rwkv_post(y_f, y_b, g, bg, ln, *, tm):
    n = y_f.shape[0]
    spec = pl.BlockSpec((tm, PAIR), lambda i, p: (i, p))
    return pl.pallas_call(
        _rwkv_post_kernel,
        out_shape=jax.ShapeDtypeStruct((n, RWKV_WIDTH), BF16),
        grid=(n // tm, N_PAIRS),
        in_specs=[spec] * 4 + [pl.BlockSpec((2, PAIR), lambda i, p: (0, p)),
                               pl.BlockSpec((PAIR, PAIR), lambda i, p: (0, 0))],
        out_specs=spec,
        compiler_params=_params(("parallel", "parallel")),
        name="rwkv_post",
    )(y_f, y_b, g, bg, ln, _seg_matrix())


def _pad_cols(w, n):
    return jnp.pad(w, [(0, 0)] * (w.ndim - 1) + [(0, n - w.shape[-1])])


def _pack_lowrank_cols(w, main):
    parts = [w[..., :main]]
    off = main
    for _ in range(4):
        parts.append(_pad_cols(w[..., off:off + DECAY_RANK], RANK_PAD))
        off += DECAY_RANK
    parts.append(w[..., off:])
    return jnp.concatenate(parts, axis=-1)


def _mixer(zin, tabs, lw, T):
    q, k, v = qk_prep(zin, tabs, lw['q_norm'], lw['k_norm'], T=T, tm=256)
    att = attention(q, k, v, T=T, bq=256, kc=512)
    r, vv, kk, lw_f, lw_b, kd_f, kd_b, b_f, b_b, g, bg = rwkv_prep(
        zin, lw['mu'], lw['wup'], lw['aup'], lw['gup'], lw['vecs'], T=T, tm=512)
    y_f = rwkv_dir(r, lw_f, kd_f, vv, kk, b_f, T=T, tt=512, bwd=False)
    y_b = rwkv_dir(r, lw_b, kd_b, vv, kk, b_b, T=T, tt=512, bwd=True)
    return att, rwkv_post(y_f, y_b, g, bg, lw['ln'], tm=512)


def _trunk(x3, layers, final_norm):
    B, T, D = x3.shape
    x = x3.reshape(B * T, D)
    tabs = _rope_tables(T)
    for lw in layers:
        act = norm_swiglu(x, lw['ffn1_norm'], lw['g1'], lw['u1'], bm=512, bn=512)
        x = matmul_residual(act, lw['d1'], x, scale=0.5, bm=512, bn=1024)
        zin = norm_matmul(x, lw['mix_norm'], lw['w_in'], bm=512, bn=768, out_dtype=F32)
        att, rw = _mixer(zin, tabs, lw, T)
        x = matmul2_residual(att, rw, lw['w_out'], x, bm=512, bn=1024)
        act = norm_swiglu(x, lw['ffn2_norm'], lw['g2'], lw['u2'], bm=512, bn=512)
        x = matmul_residual(act, lw['d2'], x, scale=0.5, bm=512, bn=1024)
    return rmsnorm(x, final_norm, bm=256).reshape(B, T, D)


def _layer_weights(l, p):
    main = ATT_COLS + 3 * RWKV_WIDTH
    pad_rank = ((0, 0), (0, RANK_PAD - DECAY_RANK), (0, 0))
    zero = jnp.zeros((RWKV_WIDTH,), F32)
    return dict(
        ffn1_norm=p['ffn1_norm'][l], mix_norm=p['mix_norm'][l], ffn2_norm=p['ffn2_norm'][l],
        q_norm=p['q_norm'][l], k_norm=p['k_norm'][l],
        g1=_pad_cols(p['ffn1_gate'][l], D_FF_PAD).astype(BF16),
        u1=_pad_cols(p['ffn1_up'][l], D_FF_PAD).astype(BF16),
        d1=jnp.pad(p['ffn1_down'][l], ((0, D_FF_PAD - D_FF), (0, 0))).astype(BF16),
        g2=_pad_cols(p['ffn2_gate'][l], D_FF_PAD).astype(BF16),
        u2=_pad_cols(p['ffn2_up'][l], D_FF_PAD).astype(BF16),
        d2=jnp.pad(p['ffn2_down'][l], ((0, D_FF_PAD - D_FF), (0, 0))).astype(BF16),
        w_in=_pack_lowrank_cols(p['w_in'][l], main).astype(BF16),
        w_out=p['w_out'][l].astype(BF16),
        mu=_pack_lowrank_cols(p['shift_mu'][l], 3 * RWKV_WIDTH),
        wup=jnp.pad(p['decay_up'][l], pad_rank).astype(BF16),
        aup=jnp.pad(p['icl_up'][l], pad_rank).astype(BF16),
        gup=p['gate_up'][l].astype(BF16),
        vecs=jnp.stack([p['decay_w0'][l, 0], p['decay_w0'][l, 1], p['icl_a0'][l, 0], p['icl_a0'][l, 1],
                        p['rwkv_k_k'][l], p['rwkv_k_a'][l], p['rwkv_r_k'][l].reshape(-1), zero]),
        ln=jnp.stack([p['ln_x_g'][l], p['ln_x_b'][l]]),
    )


def kernel(x_prompt, x_sample, ffn1_norm, ffn1_gate, ffn1_up, ffn1_down, mix_norm, w_in, q_norm, k_norm, shift_mu, decay_w0, decay_up, icl_a0, icl_up, gate_up, rwkv_k_k, rwkv_k_a, rwkv_r_k, ln_x_g, ln_x_b, w_out, ffn2_norm, ffn2_gate, ffn2_up, ffn2_down, final_norm):
    p = dict(ffn1_norm=ffn1_norm, ffn1_gate=ffn1_gate, ffn1_up=ffn1_up, ffn1_down=ffn1_down, mix_norm=mix_norm,
             w_in=w_in, q_norm=q_norm, k_norm=k_norm, shift_mu=shift_mu, decay_w0=decay_w0, decay_up=decay_up,
             icl_a0=icl_a0, icl_up=icl_up, gate_up=gate_up, rwkv_k_k=rwkv_k_k, rwkv_k_a=rwkv_k_a,
             rwkv_r_k=rwkv_r_k, ln_x_g=ln_x_g, ln_x_b=ln_x_b, w_out=w_out, ffn2_norm=ffn2_norm,
             ffn2_gate=ffn2_gate, ffn2_up=ffn2_up, ffn2_down=ffn2_down)
    layers = [_layer_weights(l, p) for l in range(DEPTH)]
    return (_trunk(x_prompt, layers, final_norm), _trunk(x_sample, layers, final_norm))
```

```python
import functools

import jax
import jax.numpy as jnp
import numpy as np
from jax import lax
from jax.experimental import pallas as pl
from jax.experimental.pallas import tpu as pltpu

F32 = jnp.float32
BF16 = jnp.bfloat16

D_MODEL = 4096
DEPTH = 4
GRID_W = 64
ATT_HEAD_DIM = 128
ATT_WIDTH = D_MODEL // 2
ATT_HEADS = ATT_WIDTH // ATT_HEAD_DIM
ATT_KV_HEADS = ATT_HEADS // 4
ATT_KV_WIDTH = ATT_KV_HEADS * ATT_HEAD_DIM
ROPE_BASE = 10000.0
RWKV_WIDTH = D_MODEL - ATT_WIDTH
RWKV_HEAD_DIM = 64
RWKV_HEADS = RWKV_WIDTH // RWKV_HEAD_DIM
DECAY_RANK = 96
ICL_RANK = 96
GATE_RANK = 256
D_FF = 5504
NORM_EPS = 1e-6
GN_EPS = 64e-5
ATT_COLS = ATT_WIDTH + 2 * ATT_KV_WIDTH

LANES = 128
MXU_COLS = 256
VMEM_LIMIT = 56 * 1024 * 1024

D_FF_PAD = -(-D_FF // MXU_COLS) * MXU_COLS
RANK_PAD = LANES
LOWRANK_COLS = 4 * RANK_PAD + GATE_RANK
IN_COLS_PAD = ATT_COLS + 3 * RWKV_WIDTH + LOWRANK_COLS


def _params(sem):
    return pltpu.CompilerParams(dimension_semantics=sem, vmem_limit_bytes=VMEM_LIMIT)


def _rmsnorm_rows(x_ref, g_ref, h_ref, rows):
    bm = x_ref.shape[0]

    def body(i, carry):
        r = pl.multiple_of(i * rows, rows)
        x = x_ref[pl.ds(r, rows), :]
        ms = jnp.mean(x * x, axis=-1, keepdims=True)
        h_ref[pl.ds(r, rows), :] = (x * lax.rsqrt(ms + NORM_EPS) * g_ref[...]).astype(BF16)
        return carry

    lax.fori_loop(0, bm // rows, body, 0)


def _norm_matmul_kernel(x_ref, g_ref, w_ref, o_ref, h_ref):
    @pl.when(pl.program_id(1) == 0)
    def _():
        _rmsnorm_rows(x_ref, g_ref, h_ref, 64)

    o_ref[...] = jnp.dot(h_ref[...], w_ref[...], preferred_element_type=F32).astype(o_ref.dtype)


def _norm_swiglu_kernel(x_ref, g_ref, wg_ref, wu_ref, o_ref, h_ref):
    @pl.when(pl.program_id(1) == 0)
    def _():
        _rmsnorm_rows(x_ref, g_ref, h_ref, 64)

    h = h_ref[...]
    gate = jnp.dot(h, wg_ref[...], preferred_element_type=F32)
    up = jnp.dot(h, wu_ref[...], preferred_element_type=F32)
    o_ref[...] = (gate * jax.nn.sigmoid(gate) * up).astype(o_ref.dtype)


def _matmul_residual_kernel(a_ref, w_ref, r_ref, o_ref, *, scale):
    acc = jnp.dot(a_ref[...], w_ref[...], preferred_element_type=F32)
    o_ref[...] = r_ref[...] + scale * acc


def norm_matmul(x, g, w, *, bm, bn, out_dtype):
    m, k = x.shape
    n = w.shape[1]
    return pl.pallas_call(
        _norm_matmul_kernel,
        out_shape=jax.ShapeDtypeStruct((m, n), out_dtype),
        grid=(m // bm, n // bn),
        in_specs=[pl.BlockSpec((bm, k), lambda i, j: (i, 0)),
                  pl.BlockSpec((1, k), lambda i, j: (0, 0)),
                  pl.BlockSpec((k, bn), lambda i, j: (0, j))],
        out_specs=pl.BlockSpec((bm, bn), lambda i, j: (i, j)),
        scratch_shapes=[pltpu.VMEM((bm, k), BF16)],
        compiler_params=_params(("parallel", "arbitrary")),
        name="norm_matmul",
    )(x, g.reshape(1, k), w)


def norm_swiglu(x, g, wg, wu, *, bm, bn):
    m, k = x.shape
    n = wg.shape[1]
    return pl.pallas_call(
        _norm_swiglu_kernel,
        out_shape=jax.ShapeDtypeStruct((m, n), BF16),
        grid=(m // bm, n // bn),
        in_specs=[pl.BlockSpec((bm, k), lambda i, j: (i, 0)),
                  pl.BlockSpec((1, k), lambda i, j: (0, 0)),
                  pl.BlockSpec((k, bn), lambda i, j: (0, j)),
                  pl.BlockSpec((k, bn), lambda i, j: (0, j))],
        out_specs=pl.BlockSpec((bm, bn), lambda i, j: (i, j)),
        scratch_shapes=[pltpu.VMEM((bm, k), BF16)],
        compiler_params=_params(("parallel", "arbitrary")),
        name="norm_swiglu",
    )(x, g.reshape(1, k), wg, wu)


def matmul_residual(a, w, res, *, scale, bm, bn):
    m, k = a.shape
    n = w.shape[1]
    return pl.pallas_call(
        functools.partial(_matmul_residual_kernel, scale=scale),
        out_shape=jax.ShapeDtypeStruct((m, n), F32),
        grid=(m // bm, n // bn),
        in_specs=[pl.BlockSpec((bm, k), lambda i, j: (i, 0)),
                  pl.BlockSpec((k, bn), lambda i, j: (0, j)),
                  pl.BlockSpec((bm, bn), lambda i, j: (i, j))],
        out_specs=pl.BlockSpec((bm, bn), lambda i, j: (i, j)),
        input_output_aliases={2: 0},
        compiler_params=_params(("parallel", "parallel")),
        name="matmul_residual",
    )(a, w, res)


def _matmul2_residual_kernel(a1_ref, a2_ref, w_ref, r_ref, o_ref):
    k1 = a1_ref.shape[1]
    acc = jnp.dot(a1_ref[...], w_ref[:k1, :], preferred_element_type=F32)
    acc = acc + jnp.dot(a2_ref[...], w_ref[k1:, :], preferred_element_type=F32)
    o_ref[...] = r_ref[...] + acc


def matmul2_residual(a1, a2, w, res, *, bm, bn):
    m, k1 = a1.shape
    k2 = a2.shape[1]
    n = w.shape[1]
    return pl.pallas_call(
        _matmul2_residual_kernel,
        out_shape=jax.ShapeDtypeStruct((m, n), F32),
        grid=(m // bm, n // bn),
        in_specs=[pl.BlockSpec((bm, k1), lambda i, j: (i, 0)),
                  pl.BlockSpec((bm, k2), lambda i, j: (i, 0)),
                  pl.BlockSpec((k1 + k2, bn), lambda i, j: (0, j)),
                  pl.BlockSpec((bm, bn), lambda i, j: (i, j))],
        out_specs=pl.BlockSpec((bm, bn), lambda i, j: (i, j)),
        input_output_aliases={3: 0},
        compiler_params=_params(("parallel", "parallel")),
        name="matmul2_residual",
    )(a1, a2, w, res)


def _rmsnorm_kernel(x_ref, g_ref, o_ref):
    x = x_ref[...]
    ms = jnp.mean(x * x, axis=-1, keepdims=True)
    o_ref[...] = x * lax.rsqrt(ms + NORM_EPS) * g_ref[...]


def rmsnorm(x, g, *, bm):
    m, k = x.shape
    return pl.pallas_call(
        _rmsnorm_kernel,
        out_shape=jax.ShapeDtypeStruct((m, k), F32),
        grid=(m // bm,),
        in_specs=[pl.BlockSpec((bm, k), lambda i: (i, 0)),
                  pl.BlockSpec((1, k), lambda i: (0, 0))],
        out_specs=pl.BlockSpec((bm, k), lambda i: (i, 0)),
        compiler_params=_params(("parallel",)),
        name="final_rmsnorm",
    )(x, g.reshape(1, k))


def _rope_tables(T):
    half = ATT_HEAD_DIM // 2
    quarter = half // 2
    t = jnp.arange(T)
    pos = jnp.stack([(t // GRID_W), (t % GRID_W)], axis=1).astype(F32)
    inv = jnp.power(ROPE_BASE, -jnp.arange(0, half, 2, dtype=F32) / half)
    lane = np.arange(ATT_HEAD_DIM)
    ang = pos[:, lane // half] * inv[lane % quarter][None, :]
    first = jnp.asarray((lane % half) < quarter)[None, :]
    cos, sin = jnp.cos(ang), jnp.sin(ang)
    return cos, jnp.where(first, -sin, 0.0), jnp.where(first, 0.0, sin)


def _qk_prep_kernel(q_ref, k_ref, v_ref, c_ref, sa_ref, sb_ref, qg_ref, kg_ref, qo_ref, ko_ref, vo_ref):
    c, sa, sb = c_ref[...], sa_ref[...], sb_ref[...]
    quarter = ATT_HEAD_DIM // 4

    def prep(x, gain, scale, dtype=BF16):
        ms = jnp.mean(x * x, axis=-1, keepdims=True)
        y = x * lax.rsqrt(ms + NORM_EPS) * gain
        y = y * c + pltpu.roll(y, ATT_HEAD_DIM - quarter, 1) * sa + pltpu.roll(y, quarter, 1) * sb
        return (y * scale).astype(dtype)

    for h in range(ATT_HEADS):
        sl = slice(h * ATT_HEAD_DIM, (h + 1) * ATT_HEAD_DIM)
        qo_ref[:, sl] = prep(q_ref[:, sl], qg_ref[...], ATT_HEAD_DIM ** -0.5)
    for h in range(ATT_KV_HEADS):
        sl = slice(h * ATT_HEAD_DIM, (h + 1) * ATT_HEAD_DIM)
        ko_ref[sl, :] = prep(k_ref[:, sl], kg_ref[...], 1.0, F32).T.astype(BF16)
    vo_ref[...] = v_ref[...].astype(BF16)


def qk_prep(zin, tabs, q_gain, k_gain, *, T, tm):
    n = zin.shape[0]
    tps = T // tm
    kvb = ATT_WIDTH // ATT_KV_WIDTH
    tab_spec = pl.BlockSpec((tm, ATT_HEAD_DIM), lambda i: (i % tps, 0))
    vec_spec = pl.BlockSpec((1, ATT_HEAD_DIM), lambda i: (0, 0))
    return pl.pallas_call(
        _qk_prep_kernel,
        out_shape=(jax.ShapeDtypeStruct((n, ATT_WIDTH), BF16),
                   jax.ShapeDtypeStruct((ATT_KV_WIDTH, n), BF16),
                   jax.ShapeDtypeStruct((n, ATT_KV_WIDTH), BF16)),
        grid=(n // tm,),
        in_specs=[pl.BlockSpec((tm, ATT_WIDTH), lambda i: (i, 0)),
                  pl.BlockSpec((tm, ATT_KV_WIDTH), lambda i: (i, kvb)),
                  pl.BlockSpec((tm, ATT_KV_WIDTH), lambda i: (i, kvb + 1)),
                  tab_spec, tab_spec, tab_spec, vec_spec, vec_spec],
        out_specs=(pl.BlockSpec((tm, ATT_WIDTH), lambda i: (i, 0)),
                   pl.BlockSpec((ATT_KV_WIDTH, tm), lambda i: (0, i)),
                   pl.BlockSpec((tm, ATT_KV_WIDTH), lambda i: (i, 0))),
        compiler_params=_params(("parallel",)),
        name="qk_prep",
    )(zin, zin, zin, *tabs, q_gain.reshape(1, -1), k_gain.reshape(1, -1))


def _attn_kernel(q_ref, kt_ref, v_ref, o_ref, *scratch, kc, rb, ahead):
    bq = q_ref.shape[0]
    T = v_ref.shape[0]
    G = ATT_HEADS // ATT_KV_HEADS
    hd = ATT_HEAD_DIM
    nb = G * bq // rb
    m_refs, l_refs, acc_refs = scratch[:nb], scratch[nb:2 * nb], scratch[2 * nb:]
    for i in range(nb):
        m_refs[i][...] = jnp.full(m_refs[i].shape, -jnp.inf, F32)
        l_refs[i][...] = jnp.zeros(l_refs[i].shape, F32)
        acc_refs[i][...] = jnp.zeros(acc_refs[i].shape, F32)

    def body(c, carry):
        c0 = pl.multiple_of(c * kc, kc)
        kt = kt_ref[:, pl.ds(c0, kc)]
        vch = v_ref[pl.ds(c0, kc), :]

        def scores(i):
            g, r = divmod(i, bq // rb)
            return jnp.dot(q_ref[r * rb:(r + 1) * rb, g * hd:(g + 1) * hd], kt, preferred_element_type=F32)

        pending = [scores(i) for i in range(min(ahead, nb))]
        for i in range(nb):
            s = pending.pop(0)
            if i + ahead < nb:
                pending.append(scores(i + ahead))
            m_prev = m_refs[i][...]
            m_new = jnp.maximum(m_prev, jnp.max(s, axis=-1, keepdims=True))
            alpha = jnp.exp(m_prev - m_new)
            p = jnp.exp(s - m_new)
            psum = p[:, :LANES]
            for j in range(1, kc // LANES):
                psum = psum + p[:, j * LANES:(j + 1) * LANES]
            l_refs[i][...] = alpha * l_refs[i][...] + psum
            acc_refs[i][...] = alpha * acc_refs[i][...] + jnp.dot(p.astype(BF16), vch, preferred_element_type=F32)
            m_refs[i][...] = m_new
        return carry

    lax.fori_loop(0, T // kc, body, 0)
    for i in range(nb):
        g, r = divmod(i, bq // rb)
        l = jnp.sum(l_refs[i][...], axis=-1, keepdims=True)
        o_ref[r * rb:(r + 1) * rb, g * hd:(g + 1) * hd] = (acc_refs[i][...] / l).astype(o_ref.dtype)


def attention(q, kt, v, *, T, bq, kc, rb, ahead):
    n = q.shape[0]
    n_seq = n // T
    G = ATT_HEADS // ATT_KV_HEADS
    gw = G * ATT_HEAD_DIM
    nq = T // bq
    nb = G * bq // rb
    return pl.pallas_call(
        functools.partial(_attn_kernel, kc=kc, rb=rb, ahead=ahead),
        out_shape=jax.ShapeDtypeStruct((n, ATT_WIDTH), BF16),
        grid=(n_seq, ATT_KV_HEADS, nq),
        in_specs=[pl.BlockSpec((bq, gw), lambda s, h, i: (s * nq + i, h)),
                  pl.BlockSpec((ATT_HEAD_DIM, T), lambda s, h, i: (h, s)),
                  pl.BlockSpec((T, ATT_HEAD_DIM), lambda s, h, i: (s, h))],
        out_specs=pl.BlockSpec((bq, gw), lambda s, h, i: (s * nq + i, h)),
        scratch_shapes=([pltpu.VMEM((rb, 1), F32)] * nb + [pltpu.VMEM((rb, LANES), F32)] * nb
                        + [pltpu.VMEM((rb, ATT_HEAD_DIM), F32)] * nb),
        compiler_params=_params(("parallel", "parallel", "arbitrary")),
        name="attention",
    )(q, kt, v)


PAIR = 2 * RWKV_HEAD_DIM
N_PAIRS = RWKV_WIDTH // PAIR
CHUNK = 64
HI = lax.Precision.HIGHEST


def _seg_matrix():
    lane = np.arange(PAIR)
    return jnp.asarray((lane[:, None] // RWKV_HEAD_DIM == lane[None, :] // RWKV_HEAD_DIM).astype(np.float32))


def _softplus(x):
    return jnp.maximum(x, 0.0) + jnp.log(1.0 + jnp.exp(-jnp.abs(x)))


def _rwkv_prep_kernel(zr_ref, zrp_ref, zrn_ref, zk_ref, zkp_ref, zkn_ref, zv_ref, zvp_ref, zvn_ref,
                      zl_ref, zlp_ref, zln_ref, mur_ref, muk_ref, muv_ref, mul_ref,
                      wup_ref, aup_ref, gup_ref, vec_ref, seg_ref,
                      r_o, v_o, kk_o, lwf_o, lwb_o, kdf_o, kdb_o, bf_o, bb_o, g_o, bg_o,
                      lr_ref, *, tiles_per_seq):
    i = pl.program_id(0)
    tm = zr_ref.shape[0]
    first = (i % tiles_per_seq) == 0
    last = (i % tiles_per_seq) == tiles_per_seq - 1
    rows = lax.broadcasted_iota(jnp.int32, (tm, 1), 0)

    def shift(z_ref, zp_ref, zn_ref, mu_ref):
        z = z_ref[...]
        prev_row = jnp.where(first, 0.0, zp_ref[7:8, :])
        next_row = jnp.where(last, 0.0, zn_ref[0:1, :])
        zp = jnp.where(rows == 0, prev_row, pltpu.roll(z, 1, 0))
        zn = jnp.where(rows == tm - 1, next_row, pltpu.roll(z, tm - 1, 0))
        return z + mu_ref[0:1, :] * (zp - z) + mu_ref[1:2, :] * (zn - z)

    @pl.when(pl.program_id(1) == 0)
    def _():
        lr = shift(zl_ref, zlp_ref, zln_ref, mul_ref)
        nd = 2 * RANK_PAD
        lr_ref[:, :nd] = jnp.tanh(lr[:, :nd]).astype(BF16)
        lr_ref[:, nd:2 * nd] = lr[:, nd:2 * nd].astype(BF16)
        lr_ref[:, 2 * nd:] = jax.nn.sigmoid(lr[:, 2 * nd:]).astype(BF16)

    r = shift(zr_ref, zrp_ref, zrn_ref, mur_ref)
    k = shift(zk_ref, zkp_ref, zkn_ref, muk_ref)
    v = shift(zv_ref, zvp_ref, zvn_ref, muv_ref)
    seg = seg_ref[...]
    k_k, k_a, r_k = vec_ref[4:5, :], vec_ref[5:6, :], vec_ref[6:7, :]

    kk = k * k_k
    ss = jnp.dot(kk * kk, seg, preferred_element_type=F32, precision=HI)
    kk = kk * lax.rsqrt(jnp.maximum(ss, 1e-24))

    def direction(d):
        tw = lr_ref[:, d * RANK_PAD:(d + 1) * RANK_PAD]
        ad = lr_ref[:, (2 + d) * RANK_PAD:(3 + d) * RANK_PAD]
        u = vec_ref[d:d + 1, :] + jnp.dot(tw, wup_ref[d], preferred_element_type=F32)
        w_log = -_softplus(-u) - 0.5
        lw = -jnp.exp(w_log)
        a = jax.nn.sigmoid(vec_ref[2 + d:3 + d, :] + jnp.dot(ad, aup_ref[d], preferred_element_type=F32))
        kd = k * (1.0 + (a - 1.0) * k_a)
        return lw, kd, kk * a

    lw_f, kd_f, b_f = direction(0)
    lw_b, kd_b, b_b = direction(1)
    g = jnp.dot(lr_ref[:, 4 * RANK_PAD:], gup_ref[...], preferred_element_type=F32)
    bonus = jnp.dot(r * kd_f * r_k, seg, preferred_element_type=F32, precision=HI) * v
    r_o[...] = r
    v_o[...] = v
    kk_o[...] = kk
    lwf_o[...] = lw_f
    lwb_o[...] = lw_b
    kdf_o[...] = kd_f
    kdb_o[...] = kd_b
    bf_o[...] = b_f
    bb_o[...] = b_b
    g_o[...] = g
    bg_o[...] = bonus * g


def rwkv_prep(zin, mu, wup, aup, gup, vecs, *, T, tm):
    n = zin.shape[0]
    tps = T // tm
    hb = tm // 8
    nb8 = n // 8
    c_r = ATT_COLS // PAIR
    c_k = c_r + N_PAIRS
    c_v = c_k + N_PAIRS
    c_l = (ATT_COLS + 3 * RWKV_WIDTH) // LOWRANK_COLS
    assert c_l * LOWRANK_COLS == ATT_COLS + 3 * RWKV_WIDTH

    def main_specs(c0):
        return [pl.BlockSpec((tm, PAIR), lambda i, p: (i, c0 + p)),
                pl.BlockSpec((8, PAIR), lambda i, p: (jnp.maximum(i * hb - 1, 0), c0 + p)),
                pl.BlockSpec((8, PAIR), lambda i, p: (jnp.minimum((i + 1) * hb, nb8 - 1), c0 + p))]

    low_specs = [pl.BlockSpec((tm, LOWRANK_COLS), lambda i, p: (i, c_l)),
                 pl.BlockSpec((8, LOWRANK_COLS), lambda i, p: (jnp.maximum(i * hb - 1, 0), c_l)),
                 pl.BlockSpec((8, LOWRANK_COLS), lambda i, p: (jnp.minimum((i + 1) * hb, nb8 - 1), c_l))]
    mu_specs = [pl.BlockSpec((2, PAIR), lambda i, p: (0, p)),
                pl.BlockSpec((2, PAIR), lambda i, p: (0, N_PAIRS + p)),
                pl.BlockSpec((2, PAIR), lambda i, p: (0, 2 * N_PAIRS + p)),
                pl.BlockSpec((2, LOWRANK_COLS), lambda i, p: (0, 3 * RWKV_WIDTH // LOWRANK_COLS))]
    w_specs = [pl.BlockSpec((2, RANK_PAD, PAIR), lambda i, p: (0, 0, p)),
               pl.BlockSpec((2, RANK_PAD, PAIR), lambda i, p: (0, 0, p)),
               pl.BlockSpec((GATE_RANK, PAIR), lambda i, p: (0, p)),
               pl.BlockSpec((8, PAIR), lambda i, p: (0, p)),
               pl.BlockSpec((PAIR, PAIR), lambda i, p: (0, 0))]
    out_spec = pl.BlockSpec((tm, PAIR), lambda i, p: (i, p))
    n_out = 11
    return pl.pallas_call(
        functools.partial(_rwkv_prep_kernel, tiles_per_seq=tps),
        out_shape=tuple(jax.ShapeDtypeStruct((n, RWKV_WIDTH), F32) for _ in range(n_out)),
        grid=(n // tm, N_PAIRS),
        in_specs=main_specs(c_r) + main_specs(c_k) + main_specs(c_v) + low_specs + mu_specs + w_specs,
        out_specs=tuple(out_spec for _ in range(n_out)),
        scratch_shapes=[pltpu.VMEM((tm, LOWRANK_COLS), BF16)],
        compiler_params=_params(("parallel", "arbitrary")),
        name="rwkv_prep",
    )(*([zin] * 12), mu, mu, mu, mu, wup, aup, gup, vecs, _seg_matrix())


def _chunk_masks():
    idx = np.arange(PAIR)
    h, t = idx // CHUNK, idx % CHUNK
    same = h[:, None] == h[None, :]
    tc = np.arange(CHUNK)

    def bd(w):
        return same & (t[:, None] // w == t[None, :] // w)

    lv = [bd(8), bd(16) & ~bd(8), bd(32) & ~bd(16), same & ~bd(32)]
    masks, cums = [], []
    for bwd in (False, True):
        tt, ss = (t[None, :], t[:, None]) if bwd else (t[:, None], t[None, :])
        strict = same & (tt > ss)
        read = strict if bwd else same & (tt >= ss)
        masks.append(np.stack([strict, read] + lv + [np.eye(PAIR, dtype=bool)]))
        cums.append((tc[:, None] <= tc[None, :]) if bwd else (tc[:, None] >= tc[None, :]))
    return jnp.asarray(np.stack(masks).astype(np.float32)), jnp.asarray(np.stack(cums).astype(np.float32)).astype(BF16)


def _bdot(a, b):
    return jnp.dot(a.astype(BF16), b.astype(BF16), preferred_element_type=F32)


def _bdot_nt(a, b):
    return lax.dot_general(a.astype(BF16), b.astype(BF16), (((1,), (1,)), ((), ())), preferred_element_type=F32)


def _bdot_tn(a, b):
    return lax.dot_general(a.astype(BF16), b.astype(BF16), (((0,), (0,)), ((), ())), preferred_element_type=F32)


def _rwkv_chunks_kernel(rf_ref, lwf_ref, kdf_ref, vf_ref, kkf_ref, bf_ref,
                        rb_ref, lwb_ref, kdb_ref, vb_ref, kkb_ref, bb_ref,
                        mask_ref, cum_ref, yf_ref, yb_ref, s_ref):
    n_chunks = rf_ref.shape[0] // CHUNK
    pg = rf_ref.shape[1] // PAIR
    P = PAIR
    ins = ((rf_ref, lwf_ref, kdf_ref, vf_ref, kkf_ref, bf_ref), (rb_ref, lwb_ref, kdb_ref, vb_ref, kkb_ref, bb_ref))
    outs = (yf_ref, yb_ref)
    chains = [(d, g) for d in range(2) for g in range(pg)]

    @pl.when(pl.program_id(2) == 0)
    def _():
        s_ref[...] = jnp.zeros(s_ref.shape, F32)

    head_a = lax.broadcasted_iota(jnp.int32, (1, P), 1) < RWKV_HEAD_DIM

    def stack(x):
        return jnp.concatenate([jnp.where(head_a, x, 0.0), jnp.where(head_a, 0.0, x)], axis=0)

    def pair(x, g):
        return x[:, g * P:(g + 1) * P]

    def chunk(ci, carry):
        slabs = []
        for d in range(2):
            bwd = d == 1
            c = (n_chunks - 1 - ci) if bwd else ci
            sl = pl.ds(pl.multiple_of(c * CHUNK, CHUNK), CHUNK)
            r, lw, kd, v, kk, b = (ref[sl, :] for ref in ins[d])
            mid = CHUNK // 2 if bwd else CHUNK // 2 - 1
            end = 0 if bwd else CHUNK - 1
            hi = lw.astype(BF16)
            r1 = lw - hi.astype(F32)
            md = r1.astype(BF16)
            lo = (r1 - md.astype(F32)).astype(BF16)
            cm = cum_ref[d]
            cum = (jnp.dot(cm, hi, preferred_element_type=F32) + jnp.dot(cm, md, preferred_element_type=F32)
                   + jnp.dot(cm, lo, preferred_element_type=F32))
            cex = cum - lw
            m = cum[mid:mid + 1, :]
            cend = cum[end:end + 1, :]
            e1 = jnp.exp(cex - m)
            e2 = jnp.exp(m - cum)
            e3 = e1 if bwd else jnp.exp(cum - m)
            em = jnp.exp(m)
            eem = jnp.exp(cend - m)
            a_q = -kk * e1
            b_k = b * e2
            k_k = kd * e2
            r_q = r * e3
            slabs.append(dict(sl=sl, a_q=a_q, b_k=b_k, k_k=k_k, r_q=r_q, a_0=a_q * em, r_0=r_q * em,
                              b_e=b_k * eem, k_e=k_k * eem, v=v, we=jnp.exp(cend)))

        def per_chain(fn):
            return [fn(d, g) for d, g in chains]

        sc = per_chain(lambda d, g: _bdot_nt(
            jnp.concatenate([stack(pair(slabs[d]['a_q'], g)), stack(pair(slabs[d]['r_q'], g))], axis=0),
            jnp.concatenate([stack(pair(slabs[d]['b_k'], g)), stack(pair(slabs[d]['k_k'], g))], axis=0)))
        A = [sc[i][:P, :P] * mask_ref[d, 0] for i, (d, g) in enumerate(chains)]
        vst = per_chain(lambda d, g: stack(pair(slabs[d]['v'], g)))
        D = [A[i] * mask_ref[d, 2] for i, (d, g) in enumerate(chains)]
        D2 = [_bdot(x, x) for x in D]
        bv = [_bdot(jnp.concatenate([sc[i][:P, P:] * mask_ref[d, 0], sc[i][P:, P:] * mask_ref[d, 1]], axis=0), vst[i])
              for i, (d, g) in enumerate(chains)]
        T1 = [_bdot(D2[i], D[i]) for i in range(len(chains))]
        D4 = [_bdot(x, x) for x in D2]
        X = [mask_ref[d, 6] + D[i] + D2[i] + T1[i] for i, (d, g) in enumerate(chains)]
        X = [X[i] + _bdot(D4[i], X[i]) for i in range(len(chains))]
        for lvl in (3, 4, 5):
            T = [_bdot(X[i], A[i] * mask_ref[d, lvl]) for i, (d, g) in enumerate(chains)]
            X = [X[i] + _bdot(T[i], X[i]) for i in range(len(chains))]
        pq = [_bdot(X[i], jnp.concatenate([stack(pair(slabs[d]['a_0'], g)), bv[i][:P]], axis=1))
              for i, (d, g) in enumerate(chains)]
        S = [s_ref[d, g] for d, g in chains]
        ps = [_bdot_nt(jnp.concatenate([pq[i][:, :P], stack(pair(slabs[d]['r_0'], g))], axis=0), S[i])
              for i, (d, g) in enumerate(chains)]
        U = [ps[i][:P] + pq[i][:, P:] for i in range(len(chains))]
        MbU = [_bdot(sc[i][P:, :P] * mask_ref[d, 1], U[i]) for i, (d, g) in enumerate(chains)]
        Sn = [_bdot_tn(jnp.concatenate([U[i], vst[i]], axis=0),
                       jnp.concatenate([stack(pair(slabs[d]['b_e'], g)), stack(pair(slabs[d]['k_e'], g))], axis=0))
              for i, (d, g) in enumerate(chains)]
        for i, (d, g) in enumerate(chains):
            Y = ps[i][P:] + MbU[i] + bv[i][P:]
            outs[d][slabs[d]['sl'], g * P:(g + 1) * P] = Y[:CHUNK] + Y[CHUNK:]
            s_ref[d, g] = S[i] * pair(slabs[d]['we'], g) + Sn[i]
        return carry

    lax.fori_loop(0, n_chunks, chunk, 0)


def rwkv_chunks(r, v, kk, lw_f, kd_f, b_f, lw_b, kd_b, b_b, *, T, tt, pg):
    n = r.shape[0]
    n_seq = n // T
    nt = T // tt
    masks, cum = _chunk_masks()
    w = pg * PAIR
    fwd = pl.BlockSpec((tt, w), lambda s, p, j: (s * nt + j, p))
    bwd = pl.BlockSpec((tt, w), lambda s, p, j: (s * nt + nt - 1 - j, p))
    return pl.pallas_call(
        _rwkv_chunks_kernel,
        out_shape=(jax.ShapeDtypeStruct((n, RWKV_WIDTH), F32), jax.ShapeDtypeStruct((n, RWKV_WIDTH), F32)),
        grid=(n_seq, N_PAIRS // pg, nt),
        in_specs=[fwd] * 6 + [bwd] * 6 + [pl.BlockSpec(masks.shape, lambda s, p, j: (0, 0, 0, 0)),
                                          pl.BlockSpec(cum.shape, lambda s, p, j: (0, 0, 0))],
        out_specs=(fwd, bwd),
        scratch_shapes=[pltpu.VMEM((2, pg, PAIR, PAIR), F32)],
        compiler_params=_params(("parallel", "parallel", "arbitrary")),
        name="rwkv_chunks",
    )(r, lw_f, kd_f, v, kk, b_f, r, lw_b, kd_b, v, kk, b_b, masks, cum)


def _rwkv_post_kernel(yf_ref, yb_ref, g_ref, bg_ref, ln_ref, seg_ref, o_ref):
    y = yf_ref[...] + yb_ref[...]
    segm = seg_ref[...] * (1.0 / RWKV_HEAD_DIM)
    mu = jnp.dot(y, segm, preferred_element_type=F32, precision=HI)
    d = y - mu
    var = jnp.dot(d * d, segm, preferred_element_type=F32, precision=HI)
    yn = d * lax.rsqrt(var + GN_EPS) * ln_ref[0:1, :] + ln_ref[1:2, :]
    o_ref[...] = (yn * g_ref[...] + bg_ref[...]).astype(o_ref.dtype)


def rwkv_post(y_f, y_b, g, bg, ln, *, tm):
    n = y_f.shape[0]
    spec = pl.BlockSpec((tm, PAIR), lambda i, p: (i, p))
    return pl.pallas_call(
        _rwkv_post_kernel,
        out_shape=jax.ShapeDtypeStruct((n, RWKV_WIDTH), BF16),
        grid=(n // tm, N_PAIRS),
        in_specs=[spec] * 4 + [pl.BlockSpec((2, PAIR), lambda i, p: (0, p)),
                               pl.BlockSpec((PAIR, PAIR), lambda i, p: (0, 0))],
        out_specs=spec,
        compiler_params=_params(("parallel", "parallel")),
        name="rwkv_post",
    )(y_f, y_b, g, bg, ln, _seg_matrix())


def _pad_cols(w, n):
    return jnp.pad(w, [(0, 0)] * (w.ndim - 1) + [(0, n - w.shape[-1])])


def _pack_lowrank_cols(w, main):
    parts = [w[..., :main]]
    off = main
    for _ in range(4):
        parts.append(_pad_cols(w[..., off:off + DECAY_RANK], RANK_PAD))
        off += DECAY_RANK
    parts.append(w[..., off:])
    return jnp.concatenate(parts, axis=-1)


def _mixer(zin, tabs, lw, T):
    q, k, v = qk_prep(zin, tabs, lw['q_norm'], lw['k_norm'], T=T, tm=256)
    att = attention(q, k, v, T=T, bq=256, kc=2048, rb=128, ahead=2)
    r, vv, kk, lw_f, lw_b, kd_f, kd_b, b_f, b_b, g, bg = rwkv_prep(
        zin, lw['mu'], lw['wup'], lw['aup'], lw['gup'], lw['vecs'], T=T, tm=512)
    y_f, y_b = rwkv_chunks(r, vv, kk, lw_f, kd_f, b_f, lw_b, kd_b, b_b, T=T, tt=512, pg=4)
    return att, rwkv_post(y_f, y_b, g, bg, lw['ln'], tm=512)


def _trunk(x3, layers, final_norm):
    B, T, D = x3.shape
    x = x3.reshape(B * T, D)
    tabs = _rope_tables(T)
    for lw in layers:
        act = norm_swiglu(x, lw['ffn1_norm'], lw['g1'], lw['u1'], bm=512, bn=512)
        x = matmul_residual(act, lw['d1'], x, scale=0.5, bm=512, bn=1024)
        zin = norm_matmul(x, lw['mix_norm'], lw['w_in'], bm=512, bn=768, out_dtype=F32)
        att, rw = _mixer(zin, tabs, lw, T)
        x = matmul2_residual(att, rw, lw['w_out'], x, bm=512, bn=1024)
        act = norm_swiglu(x, lw['ffn2_norm'], lw['g2'], lw['u2'], bm=512, bn=512)
        x = matmul_residual(act, lw['d2'], x, scale=0.5, bm=512, bn=1024)
    return rmsnorm(x, final_norm, bm=256).reshape(B, T, D)


def _layer_weights(l, p):
    main = ATT_COLS + 3 * RWKV_WIDTH
    pad_rank = ((0, 0), (0, RANK_PAD - DECAY_RANK), (0, 0))
    zero = jnp.zeros((RWKV_WIDTH,), F32)
    return dict(
        ffn1_norm=p['ffn1_norm'][l], mix_norm=p['mix_norm'][l], ffn2_norm=p['ffn2_norm'][l],
        q_norm=p['q_norm'][l], k_norm=p['k_norm'][l],
        g1=_pad_cols(p['ffn1_gate'][l], D_FF_PAD).astype(BF16),
        u1=_pad_cols(p['ffn1_up'][l], D_FF_PAD).astype(BF16),
        d1=jnp.pad(p['ffn1_down'][l], ((0, D_FF_PAD - D_FF), (0, 0))).astype(BF16),
        g2=_pad_cols(p['ffn2_gate'][l], D_FF_PAD).astype(BF16),
        u2=_pad_cols(p['ffn2_up'][l], D_FF_PAD).astype(BF16),
        d2=jnp.pad(p['ffn2_down'][l], ((0, D_FF_PAD - D_FF), (0, 0))).astype(BF16),
        w_in=_pack_lowrank_cols(p['w_in'][l], main).astype(BF16),
        w_out=p['w_out'][l].astype(BF16),
        mu=_pack_lowrank_cols(p['shift_mu'][l], 3 * RWKV_WIDTH),
        wup=jnp.pad(p['decay_up'][l], pad_rank).astype(BF16),
        aup=jnp.pad(p['icl_up'][l], pad_rank).astype(BF16),
        gup=p['gate_up'][l].astype(BF16),
        vecs=jnp.stack([p['decay_w0'][l, 0], p['decay_w0'][l, 1], p['icl_a0'][l, 0], p['icl_a0'][l, 1],
                        p['rwkv_k_k'][l], p['rwkv_k_a'][l], p['rwkv_r_k'][l].reshape(-1), zero]),
        ln=jnp.stack([p['ln_x_g'][l], p['ln_x_b'][l]]),
    )


def kernel(x_prompt, x_sample, ffn1_norm, ffn1_gate, ffn1_up, ffn1_down, mix_norm, w_in, q_norm, k_norm, shift_mu, decay_w0, decay_up, icl_a0, icl_up, gate_up, rwkv_k_k, rwkv_k_a, rwkv_r_k, ln_x_g, ln_x_b, w_out, ffn2_norm, ffn2_gate, ffn2_up, ffn2_down, final_norm):
    p = dict(ffn1_norm=ffn1_norm, ffn1_gate=ffn1_gate, ffn1_up=ffn1_up, ffn1_down=ffn1_down, mix_norm=mix_norm,
             w_in=w_in, q_norm=q_norm, k_norm=k_norm, shift_mu=shift_mu, decay_w0=decay_w0, decay_up=decay_up,
             icl_a0=icl_a0, icl_up=icl_up, gate_up=gate_up, rwkv_k_k=rwkv_k_k, rwkv_k_a=rwkv_k_a,
             rwkv_r_k=rwkv_r_k, ln_x_g=ln_x_g, ln_x_b=ln_x_b, w_out=w_out, ffn2_norm=ffn2_norm,
             ffn2_gate=ffn2_gate, ffn2_up=ffn2_up, ffn2_down=ffn2_down)
    layers = [_layer_weights(l, p) for l in range(DEPTH)]
    return (_trunk(x_prompt, layers, final_norm), _trunk(x_sample, layers, final_norm))
```

```python
import functools

import jax
import jax.numpy as jnp
import numpy as np
from jax import lax
from jax.experimental import pallas as pl
from jax.experimental.pallas import tpu as pltpu

F32 = jnp.float32
BF16 = jnp.bfloat16

D_MODEL = 4096
DEPTH = 4
GRID_W = 64
ATT_HEAD_DIM = 128
ATT_WIDTH = D_MODEL // 2
ATT_HEADS = ATT_WIDTH // ATT_HEAD_DIM
ATT_KV_HEADS = ATT_HEADS // 4
ATT_KV_WIDTH = ATT_KV_HEADS * ATT_HEAD_DIM
ROPE_BASE = 10000.0
RWKV_WIDTH = D_MODEL - ATT_WIDTH
RWKV_HEAD_DIM = 64
RWKV_HEADS = RWKV_WIDTH // RWKV_HEAD_DIM
DECAY_RANK = 96
ICL_RANK = 96
GATE_RANK = 256
D_FF = 5504
NORM_EPS = 1e-6
GN_EPS = 64e-5
ATT_COLS = ATT_WIDTH + 2 * ATT_KV_WIDTH

LANES = 128
MXU_COLS = 256
VMEM_LIMIT = 56 * 1024 * 1024

D_FF_PAD = -(-D_FF // MXU_COLS) * MXU_COLS
RANK_PAD = LANES
LOWRANK_COLS = 4 * RANK_PAD + GATE_RANK
IN_COLS_PAD = ATT_COLS + 3 * RWKV_WIDTH + LOWRANK_COLS


def _params(sem):
    return pltpu.CompilerParams(dimension_semantics=sem, vmem_limit_bytes=VMEM_LIMIT)


def _rmsnorm_rows(x_ref, g_ref, h_ref, rows):
    bm = x_ref.shape[0]

    def body(i, carry):
        r = pl.multiple_of(i * rows, rows)
        x = x_ref[pl.ds(r, rows), :]
        ms = jnp.mean(x * x, axis=-1, keepdims=True)
        h_ref[pl.ds(r, rows), :] = (x * lax.rsqrt(ms + NORM_EPS) * g_ref[...]).astype(BF16)
        return carry

    lax.fori_loop(0, bm // rows, body, 0)


def _norm_matmul_kernel(x_ref, g_ref, w_ref, o_ref, h_ref):
    @pl.when(pl.program_id(1) == 0)
    def _():
        _rmsnorm_rows(x_ref, g_ref, h_ref, 64)

    o_ref[...] = jnp.dot(h_ref[...], w_ref[...], preferred_element_type=F32).astype(o_ref.dtype)


def _norm_swiglu_kernel(x_ref, g_ref, wg_ref, wu_ref, o_ref, h_ref):
    @pl.when(pl.program_id(1) == 0)
    def _():
        _rmsnorm_rows(x_ref, g_ref, h_ref, 64)

    h = h_ref[...]
    gate = jnp.dot(h, wg_ref[...], preferred_element_type=F32)
    up = jnp.dot(h, wu_ref[...], preferred_element_type=F32)
    o_ref[...] = (gate * jax.nn.sigmoid(gate) * up).astype(o_ref.dtype)


def _matmul_residual_kernel(a_ref, w_ref, r_ref, o_ref, *, scale):
    acc = jnp.dot(a_ref[...], w_ref[...], preferred_element_type=F32)
    o_ref[...] = r_ref[...] + scale * acc


def norm_matmul(x, g, w, *, bm, bn, out_dtype):
    m, k = x.shape
    n = w.shape[1]
    return pl.pallas_call(
        _norm_matmul_kernel,
        out_shape=jax.ShapeDtypeStruct((m, n), out_dtype),
        grid=(m // bm, n // bn),
        in_specs=[pl.BlockSpec((bm, k), lambda i, j: (i, 0)),
                  pl.BlockSpec((1, k), lambda i, j: (0, 0)),
                  pl.BlockSpec((k, bn), lambda i, j: (0, j))],
        out_specs=pl.BlockSpec((bm, bn), lambda i, j: (i, j)),
        scratch_shapes=[pltpu.VMEM((bm, k), BF16)],
        compiler_params=_params(("parallel", "arbitrary")),
        name="norm_matmul",
    )(x, g.reshape(1, k), w)


def norm_swiglu(x, g, wg, wu, *, bm, bn):
    m, k = x.shape
    n = wg.shape[1]
    return pl.pallas_call(
        _norm_swiglu_kernel,
        out_shape=jax.ShapeDtypeStruct((m, n), BF16),
        grid=(m // bm, n // bn),
        in_specs=[pl.BlockSpec((bm, k), lambda i, j: (i, 0)),
                  pl.BlockSpec((1, k), lambda i, j: (0, 0)),
                  pl.BlockSpec((k, bn), lambda i, j: (0, j)),
                  pl.BlockSpec((k, bn), lambda i, j: (0, j))],
        out_specs=pl.BlockSpec((bm, bn), lambda i, j: (i, j)),
        scratch_shapes=[pltpu.VMEM((bm, k), BF16)],
        compiler_params=_params(("parallel", "arbitrary")),
        name="norm_swiglu",
    )(x, g.reshape(1, k), wg, wu)


def matmul_residual(a, w, res, *, scale, bm, bn):
    m, k = a.shape
    n = w.shape[1]
    return pl.pallas_call(
        functools.partial(_matmul_residual_kernel, scale=scale),
        out_shape=jax.ShapeDtypeStruct((m, n), F32),
        grid=(m // bm, n // bn),
        in_specs=[pl.BlockSpec((bm, k), lambda i, j: (i, 0)),
                  pl.BlockSpec((k, bn), lambda i, j: (0, j)),
                  pl.BlockSpec((bm, bn), lambda i, j: (i, j))],
        out_specs=pl.BlockSpec((bm, bn), lambda i, j: (i, j)),
        input_output_aliases={2: 0},
        compiler_params=_params(("parallel", "parallel")),
        name="matmul_residual",
    )(a, w, res)


def _matmul2_residual_kernel(a1_ref, a2_ref, w_ref, r_ref, o_ref):
    k1 = a1_ref.shape[1]
    acc = jnp.dot(a1_ref[...], w_ref[:k1, :], preferred_element_type=F32)
    acc = acc + jnp.dot(a2_ref[...], w_ref[k1:, :], preferred_element_type=F32)
    o_ref[...] = r_ref[...] + acc


def matmul2_residual(a1, a2, w, res, *, bm, bn):
    m, k1 = a1.shape
    k2 = a2.shape[1]
    n = w.shape[1]
    return pl.pallas_call(
        _matmul2_residual_kernel,
        out_shape=jax.ShapeDtypeStruct((m, n), F32),
        grid=(m // bm, n // bn),
        in_specs=[pl.BlockSpec((bm, k1), lambda i, j: (i, 0)),
                  pl.BlockSpec((bm, k2), lambda i, j: (i, 0)),
                  pl.BlockSpec((k1 + k2, bn), lambda i, j: (0, j)),
                  pl.BlockSpec((bm, bn), lambda i, j: (i, j))],
        out_specs=pl.BlockSpec((bm, bn), lambda i, j: (i, j)),
        input_output_aliases={3: 0},
        compiler_params=_params(("parallel", "parallel")),
        name="matmul2_residual",
    )(a1, a2, w, res)


def _rmsnorm_kernel(x_ref, g_ref, o_ref):
    x = x_ref[...]
    ms = jnp.mean(x * x, axis=-1, keepdims=True)
    o_ref[...] = x * lax.rsqrt(ms + NORM_EPS) * g_ref[...]


def rmsnorm(x, g, *, bm):
    m, k = x.shape
    return pl.pallas_call(
        _rmsnorm_kernel,
        out_shape=jax.ShapeDtypeStruct((m, k), F32),
        grid=(m // bm,),
        in_specs=[pl.BlockSpec((bm, k), lambda i: (i, 0)),
                  pl.BlockSpec((1, k), lambda i: (0, 0))],
        out_specs=pl.BlockSpec((bm, k), lambda i: (i, 0)),
        compiler_params=_params(("parallel",)),
        name="final_rmsnorm",
    )(x, g.reshape(1, k))


def _rope_tables(T):
    half = ATT_HEAD_DIM // 2
    quarter = half // 2
    t = jnp.arange(T)
    pos = jnp.stack([(t // GRID_W), (t % GRID_W)], axis=1).astype(F32)
    inv = jnp.power(ROPE_BASE, -jnp.arange(0, half, 2, dtype=F32) / half)
    lane = np.arange(ATT_HEAD_DIM)
    ang = pos[:, lane // half] * inv[lane % quarter][None, :]
    first = jnp.asarray((lane % half) < quarter)[None, :]
    cos, sin = jnp.cos(ang), jnp.sin(ang)
    return cos, jnp.where(first, -sin, 0.0), jnp.where(first, 0.0, sin)


def _qk_prep_kernel(q_ref, k_ref, v_ref, c_ref, sa_ref, sb_ref, qg_ref, kg_ref, qo_ref, ko_ref, vo_ref):
    c, sa, sb = c_ref[...], sa_ref[...], sb_ref[...]
    quarter = ATT_HEAD_DIM // 4

    def prep(x, gain, scale, dtype=BF16):
        ms = jnp.mean(x * x, axis=-1, keepdims=True)
        y = x * lax.rsqrt(ms + NORM_EPS) * gain
        y = y * c + pltpu.roll(y, ATT_HEAD_DIM - quarter, 1) * sa + pltpu.roll(y, quarter, 1) * sb
        return (y * scale).astype(dtype)

    for h in range(ATT_HEADS):
        sl = slice(h * ATT_HEAD_DIM, (h + 1) * ATT_HEAD_DIM)
        qo_ref[:, sl] = prep(q_ref[:, sl], qg_ref[...], ATT_HEAD_DIM ** -0.5)
    for h in range(ATT_KV_HEADS):
        sl = slice(h * ATT_HEAD_DIM, (h + 1) * ATT_HEAD_DIM)
        ko_ref[sl, :] = prep(k_ref[:, sl], kg_ref[...], 1.0, F32).T.astype(BF16)
    vo_ref[...] = v_ref[...].astype(BF16)


def qk_prep(zin, tabs, q_gain, k_gain, *, T, tm):
    n = zin.shape[0]
    tps = T // tm
    kvb = ATT_WIDTH // ATT_KV_WIDTH
    tab_spec = pl.BlockSpec((tm, ATT_HEAD_DIM), lambda i: (i % tps, 0))
    vec_spec = pl.BlockSpec((1, ATT_HEAD_DIM), lambda i: (0, 0))
    return pl.pallas_call(
        _qk_prep_kernel,
        out_shape=(jax.ShapeDtypeStruct((n, ATT_WIDTH), BF16),
                   jax.ShapeDtypeStruct((ATT_KV_WIDTH, n), BF16),
                   jax.ShapeDtypeStruct((n, ATT_KV_WIDTH), BF16)),
        grid=(n // tm,),
        in_specs=[pl.BlockSpec((tm, ATT_WIDTH), lambda i: (i, 0)),
                  pl.BlockSpec((tm, ATT_KV_WIDTH), lambda i: (i, kvb)),
                  pl.BlockSpec((tm, ATT_KV_WIDTH), lambda i: (i, kvb + 1)),
                  tab_spec, tab_spec, tab_spec, vec_spec, vec_spec],
        out_specs=(pl.BlockSpec((tm, ATT_WIDTH), lambda i: (i, 0)),
                   pl.BlockSpec((ATT_KV_WIDTH, tm), lambda i: (0, i)),
                   pl.BlockSpec((tm, ATT_KV_WIDTH), lambda i: (i, 0))),
        compiler_params=_params(("parallel",)),
        name="qk_prep",
    )(zin, zin, zin, *tabs, q_gain.reshape(1, -1), k_gain.reshape(1, -1))


def _attn_kernel(q_ref, kt_ref, v_ref, o_ref, *scratch, kc, rb, ahead):
    bq = q_ref.shape[0]
    T = v_ref.shape[0]
    G = ATT_HEADS // ATT_KV_HEADS
    hd = ATT_HEAD_DIM
    nb = G * bq // rb
    m_refs, l_refs, acc_refs = scratch[:nb], scratch[nb:2 * nb], scratch[2 * nb:]
    for i in range(nb):
        m_refs[i][...] = jnp.full(m_refs[i].shape, -jnp.inf, F32)
        l_refs[i][...] = jnp.zeros(l_refs[i].shape, F32)
        acc_refs[i][...] = jnp.zeros(acc_refs[i].shape, F32)

    def body(c, carry):
        c0 = pl.multiple_of(c * kc, kc)
        kt = kt_ref[:, pl.ds(c0, kc)]
        vch = v_ref[pl.ds(c0, kc), :]

        def scores(i):
            g, r = divmod(i, bq // rb)
            return jnp.dot(q_ref[r * rb:(r + 1) * rb, g * hd:(g + 1) * hd], kt, preferred_element_type=F32)

        pending = [scores(i) for i in range(min(ahead, nb))]
        for i in range(nb):
            s = pending.pop(0)
            if i + ahead < nb:
                pending.append(scores(i + ahead))
            m_prev = m_refs[i][...]
            m_new = jnp.maximum(m_prev, jnp.max(s, axis=-1, keepdims=True))
            alpha = jnp.exp(m_prev - m_new)
            p = jnp.exp(s - m_new)
            psum = p[:, :LANES]
            for j in range(1, kc // LANES):
                psum = psum + p[:, j * LANES:(j + 1) * LANES]
            l_refs[i][...] = alpha * l_refs[i][...] + psum
            acc_refs[i][...] = alpha * acc_refs[i][...] + jnp.dot(p.astype(BF16), vch, preferred_element_type=F32)
            m_refs[i][...] = m_new
        return carry

    lax.fori_loop(0, T // kc, body, 0)
    for i in range(nb):
        g, r = divmod(i, bq // rb)
        l = jnp.sum(l_refs[i][...], axis=-1, keepdims=True)
        o_ref[r * rb:(r + 1) * rb, g * hd:(g + 1) * hd] = (acc_refs[i][...] / l).astype(o_ref.dtype)


def attention(q, kt, v, *, T, bq, kc, rb, ahead):
    n = q.shape[0]
    n_seq = n // T
    G = ATT_HEADS // ATT_KV_HEADS
    gw = G * ATT_HEAD_DIM
    nq = T // bq
    nb = G * bq // rb
    return pl.pallas_call(
        functools.partial(_attn_kernel, kc=kc, rb=rb, ahead=ahead),
        out_shape=jax.ShapeDtypeStruct((n, ATT_WIDTH), BF16),
        grid=(n_seq, ATT_KV_HEADS, nq),
        in_specs=[pl.BlockSpec((bq, gw), lambda s, h, i: (s * nq + i, h)),
                  pl.BlockSpec((ATT_HEAD_DIM, T), lambda s, h, i: (h, s)),
                  pl.BlockSpec((T, ATT_HEAD_DIM), lambda s, h, i: (s, h))],
        out_specs=pl.BlockSpec((bq, gw), lambda s, h, i: (s * nq + i, h)),
        scratch_shapes=([pltpu.VMEM((rb, 1), F32)] * nb + [pltpu.VMEM((rb, LANES), F32)] * nb
                        + [pltpu.VMEM((rb, ATT_HEAD_DIM), F32)] * nb),
        compiler_params=_params(("parallel", "parallel", "arbitrary")),
        name="attention",
    )(q, kt, v)


PAIR = 2 * RWKV_HEAD_DIM
N_PAIRS = RWKV_WIDTH // PAIR
CHUNK = 64


def _seg_matrix(value=1.0):
    lane = np.arange(PAIR)
    same = lane[:, None] // RWKV_HEAD_DIM == lane[None, :] // RWKV_HEAD_DIM
    return jnp.asarray(same.astype(np.float32) * value).astype(BF16)


def _softplus(x):
    return jnp.maximum(x, 0.0) + jnp.log(1.0 + jnp.exp(-jnp.abs(x)))


def _split3(x):
    hi = x.astype(BF16)
    r1 = x - hi.astype(F32)
    md = r1.astype(BF16)
    return hi, md, (r1 - md.astype(F32)).astype(BF16)


def _seg_dot(x, seg):
    out = []
    for q in range(x.shape[1] // PAIR):
        hi, md, lo = _split3(x[:, q * PAIR:(q + 1) * PAIR])
        out.append(jnp.dot(hi, seg, preferred_element_type=F32) + jnp.dot(md, seg, preferred_element_type=F32)
                   + jnp.dot(lo, seg, preferred_element_type=F32))
    return jnp.concatenate(out, axis=1)


def _rwkv_prep_kernel(zr_ref, zrp_ref, zrn_ref, zk_ref, zkp_ref, zkn_ref, zv_ref, zvp_ref, zvn_ref,
                      zl_ref, zlp_ref, zln_ref, mur_ref, muk_ref, muv_ref, mul_ref,
                      wup_ref, aup_ref, gup_ref, vec_ref, seg_ref,
                      r_o, v_o, kk_o, lwf_o, lwb_o, kdf_o, kdb_o, bf_o, bb_o, g_o, bg_o,
                      lr_ref, *, tiles_per_seq):
    i = pl.program_id(0)
    tm = zr_ref.shape[0]
    first = (i % tiles_per_seq) == 0
    last = (i % tiles_per_seq) == tiles_per_seq - 1
    rows = lax.broadcasted_iota(jnp.int32, (tm, 1), 0)

    def shift(z_ref, zp_ref, zn_ref, mu_ref):
        z = z_ref[...]
        prev_row = jnp.where(first, 0.0, zp_ref[7:8, :])
        next_row = jnp.where(last, 0.0, zn_ref[0:1, :])
        zp = jnp.where(rows == 0, prev_row, pltpu.roll(z, 1, 0))
        zn = jnp.where(rows == tm - 1, next_row, pltpu.roll(z, tm - 1, 0))
        return z + mu_ref[0:1, :] * (zp - z) + mu_ref[1:2, :] * (zn - z)

    @pl.when(pl.program_id(1) == 0)
    def _():
        lr = shift(zl_ref, zlp_ref, zln_ref, mul_ref)
        nd = 2 * RANK_PAD
        lr_ref[:, :nd] = jnp.tanh(lr[:, :nd]).astype(BF16)
        lr_ref[:, nd:2 * nd] = lr[:, nd:2 * nd].astype(BF16)
        lr_ref[:, 2 * nd:] = jax.nn.sigmoid(lr[:, 2 * nd:]).astype(BF16)

    r = shift(zr_ref, zrp_ref, zrn_ref, mur_ref)
    k = shift(zk_ref, zkp_ref, zkn_ref, muk_ref)
    v = shift(zv_ref, zvp_ref, zvn_ref, muv_ref)
    seg = seg_ref[...]
    k_k, k_a, r_k = vec_ref[4:5, :], vec_ref[5:6, :], vec_ref[6:7, :]

    kk = k * k_k
    kk = kk * lax.rsqrt(jnp.maximum(_seg_dot(kk * kk, seg), 1e-24))

    def direction(d):
        tw = lr_ref[:, d * RANK_PAD:(d + 1) * RANK_PAD]
        ad = lr_ref[:, (2 + d) * RANK_PAD:(3 + d) * RANK_PAD]
        u = vec_ref[d:d + 1, :] + jnp.dot(tw, wup_ref[d], preferred_element_type=F32)
        w_log = -_softplus(-u) - 0.5
        lw = -jnp.exp(w_log)
        a = jax.nn.sigmoid(vec_ref[2 + d:3 + d, :] + jnp.dot(ad, aup_ref[d], preferred_element_type=F32))
        kd = k * (1.0 + (a - 1.0) * k_a)
        return lw, kd, kk * a

    lw_f, kd_f, b_f = direction(0)
    lw_b, kd_b, b_b = direction(1)
    g = jnp.dot(lr_ref[:, 4 * RANK_PAD:], gup_ref[...], preferred_element_type=F32)
    bonus = _seg_dot(r * kd_f * r_k, seg) * v
    r_o[...] = r
    v_o[...] = v
    kk_o[...] = kk
    lwf_o[...] = lw_f
    lwb_o[...] = lw_b
    kdf_o[...] = kd_f
    kdb_o[...] = kd_b
    bf_o[...] = b_f
    bb_o[...] = b_b
    g_o[...] = g
    bg_o[...] = bonus * g


def rwkv_prep(zin, mu, wup, aup, gup, vecs, *, T, tm, w):
    n = zin.shape[0]
    tps = T // tm
    hb = tm // 8
    nb8 = n // 8
    nw = RWKV_WIDTH // w
    c_r = ATT_COLS // w
    c_k = c_r + nw
    c_v = c_k + nw
    c_l = (ATT_COLS + 3 * RWKV_WIDTH) // LOWRANK_COLS
    assert c_r * w == ATT_COLS and c_l * LOWRANK_COLS == ATT_COLS + 3 * RWKV_WIDTH

    def main_specs(c0):
        return [pl.BlockSpec((tm, w), lambda i, p: (i, c0 + p)),
                pl.BlockSpec((8, w), lambda i, p: (jnp.maximum(i * hb - 1, 0), c0 + p)),
                pl.BlockSpec((8, w), lambda i, p: (jnp.minimum((i + 1) * hb, nb8 - 1), c0 + p))]

    low_specs = [pl.BlockSpec((tm, LOWRANK_COLS), lambda i, p: (i, c_l)),
                 pl.BlockSpec((8, LOWRANK_COLS), lambda i, p: (jnp.maximum(i * hb - 1, 0), c_l)),
                 pl.BlockSpec((8, LOWRANK_COLS), lambda i, p: (jnp.minimum((i + 1) * hb, nb8 - 1), c_l))]
    mu_specs = [pl.BlockSpec((2, w), lambda i, p: (0, p)),
                pl.BlockSpec((2, w), lambda i, p: (0, nw + p)),
                pl.BlockSpec((2, w), lambda i, p: (0, 2 * nw + p)),
                pl.BlockSpec((2, LOWRANK_COLS), lambda i, p: (0, 3 * RWKV_WIDTH // LOWRANK_COLS))]
    w_specs = [pl.BlockSpec((2, RANK_PAD, w), lambda i, p: (0, 0, p)),
               pl.BlockSpec((2, RANK_PAD, w), lambda i, p: (0, 0, p)),
               pl.BlockSpec((GATE_RANK, w), lambda i, p: (0, p)),
               pl.BlockSpec((8, w), lambda i, p: (0, p)),
               pl.BlockSpec((PAIR, PAIR), lambda i, p: (0, 0))]
    out_spec = pl.BlockSpec((tm, w), lambda i, p: (i, p))
    n_out = 11
    return pl.pallas_call(
        functools.partial(_rwkv_prep_kernel, tiles_per_seq=tps),
        out_shape=tuple(jax.ShapeDtypeStruct((n, RWKV_WIDTH), F32) for _ in range(n_out)),
        grid=(n // tm, nw),
        in_specs=main_specs(c_r) + main_specs(c_k) + main_specs(c_v) + low_specs + mu_specs + w_specs,
        out_specs=tuple(out_spec for _ in range(n_out)),
        scratch_shapes=[pltpu.VMEM((tm, LOWRANK_COLS), BF16)],
        compiler_params=_params(("parallel", "arbitrary")),
        name="rwkv_prep",
    )(*([zin] * 12), mu, mu, mu, mu, wup, aup, gup, vecs, _seg_matrix())


def _chunk_masks():
    idx = np.arange(PAIR)
    h, t = idx // CHUNK, idx % CHUNK
    same = h[:, None] == h[None, :]
    tc = np.arange(CHUNK)

    def bd(w):
        return same & (t[:, None] // w == t[None, :] // w)

    lv = [bd(8), bd(16) & ~bd(8), bd(32) & ~bd(16), same & ~bd(32)]
    masks, cums = [], []
    for bwd in (False, True):
        tt, ss = (t[None, :], t[:, None]) if bwd else (t[:, None], t[None, :])
        strict = same & (tt > ss)
        read = strict if bwd else same & (tt >= ss)
        masks.append(np.stack([strict, read] + lv + [np.eye(PAIR, dtype=bool)]))
        cums.append((tc[:, None] <= tc[None, :]) if bwd else (tc[:, None] >= tc[None, :]))
    return jnp.asarray(np.stack(masks).astype(np.float32)), jnp.asarray(np.stack(cums).astype(np.float32)).astype(BF16)


def _bdot(a, b):
    return jnp.dot(a.astype(BF16), b.astype(BF16), preferred_element_type=F32)


def _bdot_nt(a, b):
    return lax.dot_general(a.astype(BF16), b.astype(BF16), (((1,), (1,)), ((), ())), preferred_element_type=F32)


def _bdot_tn(a, b):
    return lax.dot_general(a.astype(BF16), b.astype(BF16), (((0,), (0,)), ((), ())), preferred_element_type=F32)


def _rwkv_chunks_kernel(rf_ref, lwf_ref, kdf_ref, vf_ref, kkf_ref, bf_ref,
                        rb_ref, lwb_ref, kdb_ref, vb_ref, kkb_ref, bb_ref,
                        mask_ref, cum_ref, yf_ref, yb_ref, s_ref):
    n_chunks = rf_ref.shape[0] // CHUNK
    pg = rf_ref.shape[1] // PAIR
    P = PAIR
    ins = ((rf_ref, lwf_ref, kdf_ref, vf_ref, kkf_ref, bf_ref), (rb_ref, lwb_ref, kdb_ref, vb_ref, kkb_ref, bb_ref))
    outs = (yf_ref, yb_ref)
    chains = [(d, g) for d in range(2) for g in range(pg)]

    @pl.when(pl.program_id(2) == 0)
    def _():
        s_ref[...] = jnp.zeros(s_ref.shape, F32)

    head_a = lax.broadcasted_iota(jnp.int32, (1, P), 1) < RWKV_HEAD_DIM

    def stack(x):
        return jnp.concatenate([jnp.where(head_a, x, 0.0), jnp.where(head_a, 0.0, x)], axis=0)

    def pair(x, g):
        return x[:, g * P:(g + 1) * P]

    def chunk(ci, carry):
        slabs = []
        for d in range(2):
            bwd = d == 1
            c = (n_chunks - 1 - ci) if bwd else ci
            sl = pl.ds(pl.multiple_of(c * CHUNK, CHUNK), CHUNK)
            r, lw, kd, v, kk, b = (ref[sl, :] for ref in ins[d])
            mid = CHUNK // 2 if bwd else CHUNK // 2 - 1
            end = 0 if bwd else CHUNK - 1
            hi, md, lo = _split3(lw)
            cm = cum_ref[d]
            cum = (jnp.dot(cm, hi, preferred_element_type=F32) + jnp.dot(cm, md, preferred_element_type=F32)
                   + jnp.dot(cm, lo, preferred_element_type=F32))
            cex = cum - lw
            m = cum[mid:mid + 1, :]
            cend = cum[end:end + 1, :]
            e1 = jnp.exp(cex - m)
            e2 = jnp.exp(m - cum)
            e3 = e1 if bwd else jnp.exp(cum - m)
            em = jnp.exp(m)
            eem = jnp.exp(cend - m)
            a_q = -kk * e1
            b_k = b * e2
            k_k = kd * e2
            r_q = r * e3
            slabs.append(dict(sl=sl, a_q=a_q, b_k=b_k, k_k=k_k, r_q=r_q, a_0=a_q * em, r_0=r_q * em,
                              b_e=b_k * eem, k_e=k_k * eem, v=v, we=jnp.exp(cend)))

        def per_chain(fn):
            return [fn(d, g) for d, g in chains]

        sc = per_chain(lambda d, g: _bdot_nt(
            jnp.concatenate([stack(pair(slabs[d]['a_q'], g)), stack(pair(slabs[d]['r_q'], g))], axis=0),
            jnp.concatenate([stack(pair(slabs[d]['b_k'], g)), stack(pair(slabs[d]['k_k'], g))], axis=0)))
        A = [sc[i][:P, :P] * mask_ref[d, 0] for i, (d, g) in enumerate(chains)]
        vst = per_chain(lambda d, g: stack(pair(slabs[d]['v'], g)))
        D = [A[i] * mask_ref[d, 2] for i, (d, g) in enumerate(chains)]
        D2 = [_bdot(x, x) for x in D]
        bv = [_bdot(jnp.concatenate([sc[i][:P, P:] * mask_ref[d, 0], sc[i][P:, P:] * mask_ref[d, 1]], axis=0), vst[i])
              for i, (d, g) in enumerate(chains)]
        T1 = [_bdot(D2[i], D[i]) for i in range(len(chains))]
        D4 = [_bdot(x, x) for x in D2]
        X = [mask_ref[d, 6] + D[i] + D2[i] + T1[i] for i, (d, g) in enumerate(chains)]
        X = [X[i] + _bdot(D4[i], X[i]) for i in range(len(chains))]
        for lvl in (3, 4, 5):
            T = [_bdot(X[i], A[i] * mask_ref[d, lvl]) for i, (d, g) in enumerate(chains)]
            X = [X[i] + _bdot(T[i], X[i]) for i in range(len(chains))]
        pq = [_bdot(X[i], jnp.concatenate([stack(pair(slabs[d]['a_0'], g)), bv[i][:P]], axis=1))
              for i, (d, g) in enumerate(chains)]
        S = [s_ref[d, g] for d, g in chains]
        ps = [_bdot_nt(jnp.concatenate([pq[i][:, :P], stack(pair(slabs[d]['r_0'], g))], axis=0), S[i])
              for i, (d, g) in enumerate(chains)]
        U = [ps[i][:P] + pq[i][:, P:] for i in range(len(chains))]
        MbU = [_bdot(sc[i][P:, :P] * mask_ref[d, 1], U[i]) for i, (d, g) in enumerate(chains)]
        Sn = [_bdot_tn(jnp.concatenate([U[i], vst[i]], axis=0),
                       jnp.concatenate([stack(pair(slabs[d]['b_e'], g)), stack(pair(slabs[d]['k_e'], g))], axis=0))
              for i, (d, g) in enumerate(chains)]
        for i, (d, g) in enumerate(chains):
            Y = ps[i][P:] + MbU[i] + bv[i][P:]
            outs[d][slabs[d]['sl'], g * P:(g + 1) * P] = Y[:CHUNK] + Y[CHUNK:]
            s_ref[d, g] = S[i] * pair(slabs[d]['we'], g) + Sn[i]
        return carry

    lax.fori_loop(0, n_chunks, chunk, 0)


def rwkv_chunks(r, v, kk, lw_f, kd_f, b_f, lw_b, kd_b, b_b, *, T, tt, pg):
    n = r.shape[0]
    n_seq = n // T
    nt = T // tt
    masks, cum = _chunk_masks()
    w = pg * PAIR
    fwd = pl.BlockSpec((tt, w), lambda s, p, j: (s * nt + j, p))
    bwd = pl.BlockSpec((tt, w), lambda s, p, j: (s * nt + nt - 1 - j, p))
    return pl.pallas_call(
        _rwkv_chunks_kernel,
        out_shape=(jax.ShapeDtypeStruct((n, RWKV_WIDTH), F32), jax.ShapeDtypeStruct((n, RWKV_WIDTH), F32)),
        grid=(n_seq, N_PAIRS // pg, nt),
        in_specs=[fwd] * 6 + [bwd] * 6 + [pl.BlockSpec(masks.shape, lambda s, p, j: (0, 0, 0, 0)),
                                          pl.BlockSpec(cum.shape, lambda s, p, j: (0, 0, 0))],
        out_specs=(fwd, bwd),
        scratch_shapes=[pltpu.VMEM((2, pg, PAIR, PAIR), F32)],
        compiler_params=_params(("parallel", "parallel", "arbitrary")),
        name="rwkv_chunks",
    )(r, lw_f, kd_f, v, kk, b_f, r, lw_b, kd_b, v, kk, b_b, masks, cum)


def _rwkv_post_kernel(yf_ref, yb_ref, g_ref, bg_ref, ln_ref, seg_ref, o_ref):
    y = yf_ref[...] + yb_ref[...]
    segm = seg_ref[...]
    d = y - _seg_dot(y, segm)
    var = _seg_dot(d * d, segm)
    yn = d * lax.rsqrt(var + GN_EPS) * ln_ref[0:1, :] + ln_ref[1:2, :]
    o_ref[...] = (yn * g_ref[...] + bg_ref[...]).astype(o_ref.dtype)


def rwkv_post(y_f, y_b, g, bg, ln, *, tm, w):
    n = y_f.shape[0]
    spec = pl.BlockSpec((tm, w), lambda i, p: (i, p))
    return pl.pallas_call(
        _rwkv_post_kernel,
        out_shape=jax.ShapeDtypeStruct((n, RWKV_WIDTH), BF16),
        grid=(n // tm, RWKV_WIDTH // w),
        in_specs=[spec] * 4 + [pl.BlockSpec((2, w), lambda i, p: (0, p)),
                               pl.BlockSpec((PAIR, PAIR), lambda i, p: (0, 0))],
        out_specs=spec,
        compiler_params=_params(("parallel", "parallel")),
        name="rwkv_post",
    )(y_f, y_b, g, bg, ln, _seg_matrix(1.0 / RWKV_HEAD_DIM))


def _pad_cols(w, n):
    return jnp.pad(w, [(0, 0)] * (w.ndim - 1) + [(0, n - w.shape[-1])])


def _pack_lowrank_cols(w, main):
    parts = [w[..., :main]]
    off = main
    for _ in range(4):
        parts.append(_pad_cols(w[..., off:off + DECAY_RANK], RANK_PAD))
        off += DECAY_RANK
    parts.append(w[..., off:])
    return jnp.concatenate(parts, axis=-1)


def _mixer(zin, tabs, lw, T):
    q, k, v = qk_prep(zin, tabs, lw['q_norm'], lw['k_norm'], T=T, tm=256)
    att = attention(q, k, v, T=T, bq=256, kc=2048, rb=128, ahead=2)
    r, vv, kk, lw_f, lw_b, kd_f, kd_b, b_f, b_b, g, bg = rwkv_prep(
        zin, lw['mu'], lw['wup'], lw['aup'], lw['gup'], lw['vecs'], T=T, tm=512, w=512)
    y_f, y_b = rwkv_chunks(r, vv, kk, lw_f, kd_f, b_f, lw_b, kd_b, b_b, T=T, tt=256, pg=8)
    return att, rwkv_post(y_f, y_b, g, bg, lw['ln'], tm=512, w=512)


def _trunk(x3, layers, final_norm):
    B, T, D = x3.shape
    x = x3.reshape(B * T, D)
    tabs = _rope_tables(T)
    for lw in layers:
        act = norm_swiglu(x, lw['ffn1_norm'], lw['g1'], lw['u1'], bm=512, bn=512)
        x = matmul_residual(act, lw['d1'], x, scale=0.5, bm=512, bn=1024)
        zin = norm_matmul(x, lw['mix_norm'], lw['w_in'], bm=512, bn=768, out_dtype=F32)
        att, rw = _mixer(zin, tabs, lw, T)
        x = matmul2_residual(att, rw, lw['w_out'], x, bm=512, bn=1024)
        act = norm_swiglu(x, lw['ffn2_norm'], lw['g2'], lw['u2'], bm=512, bn=512)
        x = matmul_residual(act, lw['d2'], x, scale=0.5, bm=512, bn=1024)
    return rmsnorm(x, final_norm, bm=256).reshape(B, T, D)


def _layer_weights(l, p):
    main = ATT_COLS + 3 * RWKV_WIDTH
    pad_rank = ((0, 0), (0, RANK_PAD - DECAY_RANK), (0, 0))
    zero = jnp.zeros((RWKV_WIDTH,), F32)
    return dict(
        ffn1_norm=p['ffn1_norm'][l], mix_norm=p['mix_norm'][l], ffn2_norm=p['ffn2_norm'][l],
        q_norm=p['q_norm'][l], k_norm=p['k_norm'][l],
        g1=_pad_cols(p['ffn1_gate'][l], D_FF_PAD).astype(BF16),
        u1=_pad_cols(p['ffn1_up'][l], D_FF_PAD).astype(BF16),
        d1=jnp.pad(p['ffn1_down'][l], ((0, D_FF_PAD - D_FF), (0, 0))).astype(BF16),
        g2=_pad_cols(p['ffn2_gate'][l], D_FF_PAD).astype(BF16),
        u2=_pad_cols(p['ffn2_up'][l], D_FF_PAD).astype(BF16),
        d2=jnp.pad(p['ffn2_down'][l], ((0, D_FF_PAD - D_FF), (0, 0))).astype(BF16),
        w_in=_pack_lowrank_cols(p['w_in'][l], main).astype(BF16),
        w_out=p['w_out'][l].astype(BF16),
        mu=_pack_lowrank_cols(p['shift_mu'][l], 3 * RWKV_WIDTH),
        wup=jnp.pad(p['decay_up'][l], pad_rank).astype(BF16),
        aup=jnp.pad(p['icl_up'][l], pad_rank).astype(BF16),
        gup=p['gate_up'][l].astype(BF16),
        vecs=jnp.stack([p['decay_w0'][l, 0], p['decay_w0'][l, 1], p['icl_a0'][l, 0], p['icl_a0'][l, 1],
                        p['rwkv_k_k'][l], p['rwkv_k_a'][l], p['rwkv_r_k'][l].reshape(-1), zero]),
        ln=jnp.stack([p['ln_x_g'][l], p['ln_x_b'][l]]),
    )


def kernel(x_prompt, x_sample, ffn1_norm, ffn1_gate, ffn1_up, ffn1_down, mix_norm, w_in, q_norm, k_norm, shift_mu, decay_w0, decay_up, icl_a0, icl_up, gate_up, rwkv_k_k, rwkv_k_a, rwkv_r_k, ln_x_g, ln_x_b, w_out, ffn2_norm, ffn2_gate, ffn2_up, ffn2_down, final_norm):
    p = dict(ffn1_norm=ffn1_norm, ffn1_gate=ffn1_gate, ffn1_up=ffn1_up, ffn1_down=ffn1_down, mix_norm=mix_norm,
             w_in=w_in, q_norm=q_norm, k_norm=k_norm, shift_mu=shift_mu, decay_w0=decay_w0, decay_up=decay_up,
             icl_a0=icl_a0, icl_up=icl_up, gate_up=gate_up, rwkv_k_k=rwkv_k_k, rwkv_k_a=rwkv_k_a,
             rwkv_r_k=rwkv_r_k, ln_x_g=ln_x_g, ln_x_b=ln_x_b, w_out=w_out, ffn2_norm=ffn2_norm,
             ffn2_gate=ffn2_gate, ffn2_up=ffn2_up, ffn2_down=ffn2_down)
    layers = [_layer_weights(l, p) for l in range(DEPTH)]
    return (_trunk(x_prompt, layers, final_norm), _trunk(x_sample, layers, final_norm))
```

```python
import functools

import jax
import jax.numpy as jnp
import numpy as np
from jax import lax
from jax.experimental import pallas as pl
from jax.experimental.pallas import tpu as pltpu

F32 = jnp.float32
BF16 = jnp.bfloat16

D_MODEL = 4096
DEPTH = 4
GRID_W = 64
ATT_HEAD_DIM = 128
ATT_WIDTH = D_MODEL // 2
ATT_HEADS = ATT_WIDTH // ATT_HEAD_DIM
ATT_KV_HEADS = ATT_HEADS // 4
ATT_KV_WIDTH = ATT_KV_HEADS * ATT_HEAD_DIM
ROPE_BASE = 10000.0
RWKV_WIDTH = D_MODEL - ATT_WIDTH
RWKV_HEAD_DIM = 64
RWKV_HEADS = RWKV_WIDTH // RWKV_HEAD_DIM
DECAY_RANK = 96
ICL_RANK = 96
GATE_RANK = 256
D_FF = 5504
NORM_EPS = 1e-6
GN_EPS = 64e-5
ATT_COLS = ATT_WIDTH + 2 * ATT_KV_WIDTH

LANES = 128
MXU_COLS = 256
VMEM_LIMIT = 56 * 1024 * 1024

D_FF_PAD = -(-D_FF // MXU_COLS) * MXU_COLS
RANK_PAD = LANES
LOWRANK_COLS = 4 * RANK_PAD + GATE_RANK
IN_COLS_PAD = ATT_COLS + 3 * RWKV_WIDTH + LOWRANK_COLS


def _params(sem):
    return pltpu.CompilerParams(dimension_semantics=sem, vmem_limit_bytes=VMEM_LIMIT)


def _rmsnorm_rows(x_ref, g_ref, h_ref, rows):
    bm, k = x_ref.shape

    def body(i, carry):
        rsl = pl.ds(pl.multiple_of(i * rows, rows), rows)
        acc = jnp.zeros((rows, LANES), F32)
        for c in range(k // LANES):
            xc = x_ref[rsl, c * LANES:(c + 1) * LANES]
            acc = acc + xc * xc
        rs = lax.rsqrt(jnp.sum(acc, axis=-1, keepdims=True) * (1.0 / k) + NORM_EPS)
        for c in range(k // LANES):
            sl = slice(c * LANES, (c + 1) * LANES)
            h_ref[rsl, sl] = (x_ref[rsl, sl] * rs * g_ref[:, sl]).astype(BF16)
        return carry

    lax.fori_loop(0, bm // rows, body, 0)


def _norm_matmul_kernel(x_ref, g_ref, w_ref, o_ref, h_ref):
    @pl.when(pl.program_id(1) == 0)
    def _():
        _rmsnorm_rows(x_ref, g_ref, h_ref, 128)

    o_ref[...] = jnp.dot(h_ref[...], w_ref[...], preferred_element_type=F32).astype(o_ref.dtype)


def _norm_swiglu_kernel(x_ref, g_ref, wg_ref, wu_ref, o_ref, h_ref):
    @pl.when(pl.program_id(1) == 0)
    def _():
        _rmsnorm_rows(x_ref, g_ref, h_ref, 128)

    h = h_ref[...]
    gate = jnp.dot(h, wg_ref[...], preferred_element_type=F32)
    up = jnp.dot(h, wu_ref[...], preferred_element_type=F32)
    o_ref[...] = (gate * jax.nn.sigmoid(gate) * up).astype(o_ref.dtype)


def _matmul_residual_kernel(a_ref, w_ref, r_ref, o_ref, *, scale):
    acc = jnp.dot(a_ref[...], w_ref[...], preferred_element_type=F32)
    o_ref[...] = r_ref[...] + scale * acc


def norm_matmul(x, g, w, *, bm, bn, out_dtype):
    m, k = x.shape
    n = w.shape[1]
    return pl.pallas_call(
        _norm_matmul_kernel,
        out_shape=jax.ShapeDtypeStruct((m, n), out_dtype),
        grid=(m // bm, n // bn),
        in_specs=[pl.BlockSpec((bm, k), lambda i, j: (i, 0)),
                  pl.BlockSpec((1, k), lambda i, j: (0, 0)),
                  pl.BlockSpec((k, bn), lambda i, j: (0, j))],
        out_specs=pl.BlockSpec((bm, bn), lambda i, j: (i, j)),
        scratch_shapes=[pltpu.VMEM((bm, k), BF16)],
        compiler_params=_params(("parallel", "arbitrary")),
        name="norm_matmul",
    )(x, g.reshape(1, k), w)


def norm_swiglu(x, g, wg, wu, *, bm, bn):
    m, k = x.shape
    n = wg.shape[1]
    return pl.pallas_call(
        _norm_swiglu_kernel,
        out_shape=jax.ShapeDtypeStruct((m, n), BF16),
        grid=(m // bm, n // bn),
        in_specs=[pl.BlockSpec((bm, k), lambda i, j: (i, 0)),
                  pl.BlockSpec((1, k), lambda i, j: (0, 0)),
                  pl.BlockSpec((k, bn), lambda i, j: (0, j)),
                  pl.BlockSpec((k, bn), lambda i, j: (0, j))],
        out_specs=pl.BlockSpec((bm, bn), lambda i, j: (i, j)),
        scratch_shapes=[pltpu.VMEM((bm, k), BF16)],
        compiler_params=_params(("parallel", "arbitrary")),
        name="norm_swiglu",
    )(x, g.reshape(1, k), wg, wu)


def matmul_residual(a, w, res, *, scale, bm, bn):
    m, k = a.shape
    n = w.shape[1]
    return pl.pallas_call(
        functools.partial(_matmul_residual_kernel, scale=scale),
        out_shape=jax.ShapeDtypeStruct((m, n), F32),
        grid=(m // bm, n // bn),
        in_specs=[pl.BlockSpec((bm, k), lambda i, j: (i, 0)),
                  pl.BlockSpec((k, bn), lambda i, j: (0, j)),
                  pl.BlockSpec((bm, bn), lambda i, j: (i, j))],
        out_specs=pl.BlockSpec((bm, bn), lambda i, j: (i, j)),
        input_output_aliases={2: 0},
        compiler_params=_params(("parallel", "parallel")),
        name="matmul_residual",
    )(a, w, res)


def _matmul2_residual_kernel(a1_ref, a2_ref, w_ref, r_ref, o_ref):
    k1 = a1_ref.shape[1]
    acc = jnp.dot(a1_ref[...], w_ref[:k1, :], preferred_element_type=F32)
    acc = acc + jnp.dot(a2_ref[...], w_ref[k1:, :], preferred_element_type=F32)
    o_ref[...] = r_ref[...] + acc


def matmul2_residual(a1, a2, w, res, *, bm, bn):
    m, k1 = a1.shape
    k2 = a2.shape[1]
    n = w.shape[1]
    return pl.pallas_call(
        _matmul2_residual_kernel,
        out_shape=jax.ShapeDtypeStruct((m, n), F32),
        grid=(m // bm, n // bn),
        in_specs=[pl.BlockSpec((bm, k1), lambda i, j: (i, 0)),
                  pl.BlockSpec((bm, k2), lambda i, j: (i, 0)),
                  pl.BlockSpec((k1 + k2, bn), lambda i, j: (0, j)),
                  pl.BlockSpec((bm, bn), lambda i, j: (i, j))],
        out_specs=pl.BlockSpec((bm, bn), lambda i, j: (i, j)),
        input_output_aliases={3: 0},
        compiler_params=_params(("parallel", "parallel")),
        name="matmul2_residual",
    )(a1, a2, w, res)


def _rmsnorm_kernel(x_ref, g_ref, o_ref):
    x = x_ref[...]
    ms = jnp.mean(x * x, axis=-1, keepdims=True)
    o_ref[...] = x * lax.rsqrt(ms + NORM_EPS) * g_ref[...]


def rmsnorm(x, g, *, bm):
    m, k = x.shape
    return pl.pallas_call(
        _rmsnorm_kernel,
        out_shape=jax.ShapeDtypeStruct((m, k), F32),
        grid=(m // bm,),
        in_specs=[pl.BlockSpec((bm, k), lambda i: (i, 0)),
                  pl.BlockSpec((1, k), lambda i: (0, 0))],
        out_specs=pl.BlockSpec((bm, k), lambda i: (i, 0)),
        compiler_params=_params(("parallel",)),
        name="final_rmsnorm",
    )(x, g.reshape(1, k))


def _rope_tables(T):
    half = ATT_HEAD_DIM // 2
    quarter = half // 2
    t = jnp.arange(T)
    pos = jnp.stack([(t // GRID_W), (t % GRID_W)], axis=1).astype(F32)
    inv = jnp.power(ROPE_BASE, -jnp.arange(0, half, 2, dtype=F32) / half)
    lane = np.arange(ATT_HEAD_DIM)
    ang = pos[:, lane // half] * inv[lane % quarter][None, :]
    first = jnp.asarray((lane % half) < quarter)[None, :]
    cos, sin = jnp.cos(ang), jnp.sin(ang)
    return cos, jnp.where(first, -sin, 0.0), jnp.where(first, 0.0, sin)


def _qk_prep_kernel(q_ref, k_ref, v_ref, c_ref, sa_ref, sb_ref, qg_ref, kg_ref, qo_ref, ko_ref, vo_ref):
    c, sa, sb = c_ref[...], sa_ref[...], sb_ref[...]
    quarter = ATT_HEAD_DIM // 4

    def prep(x, gain, scale, dtype=BF16):
        ms = jnp.mean(x * x, axis=-1, keepdims=True)
        y = x * lax.rsqrt(ms + NORM_EPS) * gain
        y = y * c + pltpu.roll(y, ATT_HEAD_DIM - quarter, 1) * sa + pltpu.roll(y, quarter, 1) * sb
        return (y * scale).astype(dtype)

    for h in range(ATT_HEADS):
        sl = slice(h * ATT_HEAD_DIM, (h + 1) * ATT_HEAD_DIM)
        qo_ref[:, sl] = prep(q_ref[:, sl], qg_ref[...], ATT_HEAD_DIM ** -0.5)
    for h in range(ATT_KV_HEADS):
        sl = slice(h * ATT_HEAD_DIM, (h + 1) * ATT_HEAD_DIM)
        ko_ref[sl, :] = prep(k_ref[:, sl], kg_ref[...], 1.0, F32).T.astype(BF16)
    vo_ref[...] = v_ref[...].astype(BF16)


def qk_prep(zin, tabs, q_gain, k_gain, *, T, tm):
    n = zin.shape[0]
    tps = T // tm
    kvb = ATT_WIDTH // ATT_KV_WIDTH
    tab_spec = pl.BlockSpec((tm, ATT_HEAD_DIM), lambda i: (i % tps, 0))
    vec_spec = pl.BlockSpec((1, ATT_HEAD_DIM), lambda i: (0, 0))
    return pl.pallas_call(
        _qk_prep_kernel,
        out_shape=(jax.ShapeDtypeStruct((n, ATT_WIDTH), BF16),
                   jax.ShapeDtypeStruct((ATT_KV_WIDTH, n), BF16),
                   jax.ShapeDtypeStruct((n, ATT_KV_WIDTH), BF16)),
        grid=(n // tm,),
        in_specs=[pl.BlockSpec((tm, ATT_WIDTH), lambda i: (i, 0)),
                  pl.BlockSpec((tm, ATT_KV_WIDTH), lambda i: (i, kvb)),
                  pl.BlockSpec((tm, ATT_KV_WIDTH), lambda i: (i, kvb + 1)),
                  tab_spec, tab_spec, tab_spec, vec_spec, vec_spec],
        out_specs=(pl.BlockSpec((tm, ATT_WIDTH), lambda i: (i, 0)),
                   pl.BlockSpec((ATT_KV_WIDTH, tm), lambda i: (0, i)),
                   pl.BlockSpec((tm, ATT_KV_WIDTH), lambda i: (i, 0))),
        compiler_params=_params(("parallel",)),
        name="qk_prep",
    )(zin, zin, zin, *tabs, q_gain.reshape(1, -1), k_gain.reshape(1, -1))


def _attn_kernel(q_ref, kt_ref, v_ref, o_ref, *scratch, kc, rb, ahead):
    bq = q_ref.shape[0]
    T = v_ref.shape[0]
    G = ATT_HEADS // ATT_KV_HEADS
    hd = ATT_HEAD_DIM
    nb = G * bq // rb
    m_refs, l_refs, acc_refs = scratch[:nb], scratch[nb:2 * nb], scratch[2 * nb:]
    for i in range(nb):
        m_refs[i][...] = jnp.full(m_refs[i].shape, -jnp.inf, F32)
        l_refs[i][...] = jnp.zeros(l_refs[i].shape, F32)
        acc_refs[i][...] = jnp.zeros(acc_refs[i].shape, F32)

    def body(c, carry):
        c0 = pl.multiple_of(c * kc, kc)
        kt = kt_ref[:, pl.ds(c0, kc)]
        vch = v_ref[pl.ds(c0, kc), :]

        def scores(i):
            g, r = divmod(i, bq // rb)
            return jnp.dot(q_ref[r * rb:(r + 1) * rb, g * hd:(g + 1) * hd], kt, preferred_element_type=F32)

        pending = [scores(i) for i in range(min(ahead, nb))]
        for i in range(nb):
            s = pending.pop(0)
            if i + ahead < nb:
                pending.append(scores(i + ahead))
            m_prev = m_refs[i][...]
            m_new = jnp.maximum(m_prev, jnp.max(s, axis=-1, keepdims=True))
            alpha = jnp.exp(m_prev - m_new)
            p = jnp.exp(s - m_new)
            psum = p[:, :LANES]
            for j in range(1, kc // LANES):
                psum = psum + p[:, j * LANES:(j + 1) * LANES]
            l_refs[i][...] = alpha * l_refs[i][...] + psum
            acc_refs[i][...] = alpha * acc_refs[i][...] + jnp.dot(p.astype(BF16), vch, preferred_element_type=F32)
            m_refs[i][...] = m_new
        return carry

    lax.fori_loop(0, T // kc, body, 0)
    for i in range(nb):
        g, r = divmod(i, bq // rb)
        l = jnp.sum(l_refs[i][...], axis=-1, keepdims=True)
        o_ref[r * rb:(r + 1) * rb, g * hd:(g + 1) * hd] = (acc_refs[i][...] / l).astype(o_ref.dtype)


def attention(q, kt, v, *, T, bq, kc, rb, ahead):
    n = q.shape[0]
    n_seq = n // T
    G = ATT_HEADS // ATT_KV_HEADS
    gw = G * ATT_HEAD_DIM
    nq = T // bq
    nb = G * bq // rb
    return pl.pallas_call(
        functools.partial(_attn_kernel, kc=kc, rb=rb, ahead=ahead),
        out_shape=jax.ShapeDtypeStruct((n, ATT_WIDTH), BF16),
        grid=(n_seq, ATT_KV_HEADS, nq),
        in_specs=[pl.BlockSpec((bq, gw), lambda s, h, i: (s * nq + i, h)),
                  pl.BlockSpec((ATT_HEAD_DIM, T), lambda s, h, i: (h, s)),
                  pl.BlockSpec((T, ATT_HEAD_DIM), lambda s, h, i: (s, h))],
        out_specs=pl.BlockSpec((bq, gw), lambda s, h, i: (s * nq + i, h)),
        scratch_shapes=([pltpu.VMEM((rb, 1), F32)] * nb + [pltpu.VMEM((rb, LANES), F32)] * nb
                        + [pltpu.VMEM((rb, ATT_HEAD_DIM), F32)] * nb),
        compiler_params=_params(("parallel", "parallel", "arbitrary")),
        name="attention",
    )(q, kt, v)


PAIR = 2 * RWKV_HEAD_DIM
N_PAIRS = RWKV_WIDTH // PAIR
CHUNK = 64
DECAY_SCALE = float(np.exp(-0.5))


def _seg_matrix(value=1.0):
    lane = np.arange(PAIR)
    same = lane[:, None] // RWKV_HEAD_DIM == lane[None, :] // RWKV_HEAD_DIM
    return jnp.asarray(same.astype(np.float32) * value).astype(BF16)


def _split3(x):
    hi = x.astype(BF16)
    r1 = x - hi.astype(F32)
    md = r1.astype(BF16)
    return hi, md, (r1 - md.astype(F32)).astype(BF16)


def _seg_dot(x, seg):
    out = []
    for q in range(x.shape[1] // PAIR):
        hi, md, lo = _split3(x[:, q * PAIR:(q + 1) * PAIR])
        out.append(jnp.dot(hi, seg, preferred_element_type=F32) + jnp.dot(md, seg, preferred_element_type=F32)
                   + jnp.dot(lo, seg, preferred_element_type=F32))
    return jnp.concatenate(out, axis=1)


def _rwkv_prep_kernel(zr_ref, zrp_ref, zrn_ref, zk_ref, zkp_ref, zkn_ref, zv_ref, zvp_ref, zvn_ref,
                      zl_ref, zlp_ref, zln_ref, mur_ref, muk_ref, muv_ref, mul_ref,
                      wup_ref, aup_ref, gup_ref, vec_ref, seg_ref,
                      r_o, v_o, kk_o, lwf_o, lwb_o, kdf_o, kdb_o, bf_o, bb_o, g_o, bg_o,
                      lr_ref, *, tiles_per_seq):
    i = pl.program_id(0)
    tm = zr_ref.shape[0]
    first = (i % tiles_per_seq) == 0
    last = (i % tiles_per_seq) == tiles_per_seq - 1
    rows = lax.broadcasted_iota(jnp.int32, (tm, 1), 0)

    def shift(z_ref, zp_ref, zn_ref, mu_ref):
        z = z_ref[...]
        prev_row = jnp.where(first, 0.0, zp_ref[7:8, :])
        next_row = jnp.where(last, 0.0, zn_ref[0:1, :])
        zp = jnp.where(rows == 0, prev_row, pltpu.roll(z, 1, 0))
        zn = jnp.where(rows == tm - 1, next_row, pltpu.roll(z, tm - 1, 0))
        mu_p, mu_n = mu_ref[0:1, :], mu_ref[1:2, :]
        return (1.0 - mu_p - mu_n) * z + mu_p * zp + mu_n * zn

    @pl.when(pl.program_id(1) == 0)
    def _():
        lr = shift(zl_ref, zlp_ref, zln_ref, mul_ref)
        nd = 2 * RANK_PAD
        lr_ref[:, :nd] = jnp.tanh(lr[:, :nd]).astype(BF16)
        lr_ref[:, nd:2 * nd] = lr[:, nd:2 * nd].astype(BF16)
        lr_ref[:, 2 * nd:] = jax.nn.sigmoid(lr[:, 2 * nd:]).astype(BF16)

    r = shift(zr_ref, zrp_ref, zrn_ref, mur_ref)
    k = shift(zk_ref, zkp_ref, zkn_ref, muk_ref)
    v = shift(zv_ref, zvp_ref, zvn_ref, muv_ref)
    seg = seg_ref[...]
    k_k, k_a, r_k = vec_ref[4:5, :], vec_ref[5:6, :], vec_ref[6:7, :]

    kk = k * k_k
    kk = kk * lax.rsqrt(jnp.maximum(_seg_dot(kk * kk, seg), 1e-24))

    def direction(d):
        tw = lr_ref[:, d * RANK_PAD:(d + 1) * RANK_PAD]
        ad = lr_ref[:, (2 + d) * RANK_PAD:(3 + d) * RANK_PAD]
        u = vec_ref[d:d + 1, :] + jnp.dot(tw, wup_ref[d], preferred_element_type=F32)
        lw = -DECAY_SCALE * jax.nn.sigmoid(u)
        a = jax.nn.sigmoid(vec_ref[2 + d:3 + d, :] + jnp.dot(ad, aup_ref[d], preferred_element_type=F32))
        kd = k * (1.0 + (a - 1.0) * k_a)
        return lw, kd, kk * a

    lw_f, kd_f, b_f = direction(0)
    lw_b, kd_b, b_b = direction(1)
    g = jnp.dot(lr_ref[:, 4 * RANK_PAD:], gup_ref[...], preferred_element_type=F32)
    bonus = _seg_dot(r * kd_f * r_k, seg) * v
    r_o[...] = r
    v_o[...] = v
    kk_o[...] = kk
    lwf_o[...] = lw_f
    lwb_o[...] = lw_b
    kdf_o[...] = kd_f
    kdb_o[...] = kd_b
    bf_o[...] = b_f
    bb_o[...] = b_b
    g_o[...] = g
    bg_o[...] = bonus * g


def rwkv_prep(zin, mu, wup, aup, gup, vecs, *, T, tm, w):
    n = zin.shape[0]
    tps = T // tm
    hb = tm // 8
    nb8 = n // 8
    nw = RWKV_WIDTH // w
    c_r = ATT_COLS // w
    c_k = c_r + nw
    c_v = c_k + nw
    c_l = (ATT_COLS + 3 * RWKV_WIDTH) // LOWRANK_COLS
    assert c_r * w == ATT_COLS and c_l * LOWRANK_COLS == ATT_COLS + 3 * RWKV_WIDTH

    def main_specs(c0):
        return [pl.BlockSpec((tm, w), lambda i, p: (i, c0 + p)),
                pl.BlockSpec((8, w), lambda i, p: (jnp.maximum(i * hb - 1, 0), c0 + p)),
                pl.BlockSpec((8, w), lambda i, p: (jnp.minimum((i + 1) * hb, nb8 - 1), c0 + p))]

    low_specs = [pl.BlockSpec((tm, LOWRANK_COLS), lambda i, p: (i, c_l)),
                 pl.BlockSpec((8, LOWRANK_COLS), lambda i, p: (jnp.maximum(i * hb - 1, 0), c_l)),
                 pl.BlockSpec((8, LOWRANK_COLS), lambda i, p: (jnp.minimum((i + 1) * hb, nb8 - 1), c_l))]
    mu_specs = [pl.BlockSpec((2, w), lambda i, p: (0, p)),
                pl.BlockSpec((2, w), lambda i, p: (0, nw + p)),
                pl.BlockSpec((2, w), lambda i, p: (0, 2 * nw + p)),
                pl.BlockSpec((2, LOWRANK_COLS), lambda i, p: (0, 3 * RWKV_WIDTH // LOWRANK_COLS))]
    w_specs = [pl.BlockSpec((2, RANK_PAD, w), lambda i, p: (0, 0, p)),
               pl.BlockSpec((2, RANK_PAD, w), lambda i, p: (0, 0, p)),
               pl.BlockSpec((GATE_RANK, w), lambda i, p: (0, p)),
               pl.BlockSpec((8, w), lambda i, p: (0, p)),
               pl.BlockSpec((PAIR, PAIR), lambda i, p: (0, 0))]
    out_spec = pl.BlockSpec((tm, w), lambda i, p: (i, p))
    n_out = 11
    return pl.pallas_call(
        functools.partial(_rwkv_prep_kernel, tiles_per_seq=tps),
        out_shape=tuple(jax.ShapeDtypeStruct((n, RWKV_WIDTH), F32) for _ in range(n_out)),
        grid=(n // tm, nw),
        in_specs=main_specs(c_r) + main_specs(c_k) + main_specs(c_v) + low_specs + mu_specs + w_specs,
        out_specs=tuple(out_spec for _ in range(n_out)),
        scratch_shapes=[pltpu.VMEM((tm, LOWRANK_COLS), BF16)],
        compiler_params=_params(("parallel", "arbitrary")),
        name="rwkv_prep",
    )(*([zin] * 12), mu, mu, mu, mu, wup, aup, gup, vecs, _seg_matrix())


def _chunk_masks():
    idx = np.arange(PAIR)
    h, t = idx // CHUNK, idx % CHUNK
    same = h[:, None] == h[None, :]
    tc = np.arange(CHUNK)

    def bd(w):
        return same & (t[:, None] // w == t[None, :] // w)

    lv = [bd(8), bd(16) & ~bd(8), bd(32) & ~bd(16), same & ~bd(32)]
    masks, cums = [], []
    for bwd in (False, True):
        tt, ss = (t[None, :], t[:, None]) if bwd else (t[:, None], t[None, :])
        strict = same & (tt > ss)
        read = strict if bwd else same & (tt >= ss)
        masks.append(np.stack([strict, read] + lv + [np.eye(PAIR, dtype=bool)]))
        cums.append((tc[:, None] <= tc[None, :]) if bwd else (tc[:, None] >= tc[None, :]))
    return jnp.asarray(np.stack(masks).astype(np.float32)), jnp.asarray(np.stack(cums).astype(np.float32)).astype(BF16)


def _bdot(a, b):
    return jnp.dot(a.astype(BF16), b.astype(BF16), preferred_element_type=F32)


def _bdot_nt(a, b):
    return lax.dot_general(a.astype(BF16), b.astype(BF16), (((1,), (1,)), ((), ())), preferred_element_type=F32)


def _bdot_tn(a, b):
    return lax.dot_general(a.astype(BF16), b.astype(BF16), (((0,), (0,)), ((), ())), preferred_element_type=F32)


def _rwkv_chunks_kernel(rf_ref, lwf_ref, kdf_ref, vf_ref, kkf_ref, bf_ref,
                        rb_ref, lwb_ref, kdb_ref, vb_ref, kkb_ref, bb_ref,
                        mask_ref, cum_ref, yf_ref, yb_ref, s_ref):
    n_chunks = rf_ref.shape[0] // CHUNK
    pg = rf_ref.shape[1] // PAIR
    P = PAIR
    ins = ((rf_ref, lwf_ref, kdf_ref, vf_ref, kkf_ref, bf_ref), (rb_ref, lwb_ref, kdb_ref, vb_ref, kkb_ref, bb_ref))
    outs = (yf_ref, yb_ref)
    chains = [(d, g) for d in range(2) for g in range(pg)]

    @pl.when(pl.program_id(2) == 0)
    def _():
        s_ref[...] = jnp.zeros(s_ref.shape, F32)

    head_a = lax.broadcasted_iota(jnp.int32, (1, P), 1) < RWKV_HEAD_DIM

    def stack(x):
        return jnp.concatenate([jnp.where(head_a, x, 0.0), jnp.where(head_a, 0.0, x)], axis=0)

    def pair(x, g):
        return x[:, g * P:(g + 1) * P]

    def chunk(ci, carry):
        slabs = []
        for d in range(2):
            bwd = d == 1
            c = (n_chunks - 1 - ci) if bwd else ci
            sl = pl.ds(pl.multiple_of(c * CHUNK, CHUNK), CHUNK)
            r, lw, kd, v, kk, b = (ref[sl, :] for ref in ins[d])
            mid = CHUNK // 2 if bwd else CHUNK // 2 - 1
            end = 0 if bwd else CHUNK - 1
            hi, md, lo = _split3(lw)
            cm = cum_ref[d]
            cum = (jnp.dot(cm, hi, preferred_element_type=F32) + jnp.dot(cm, md, preferred_element_type=F32)
                   + jnp.dot(cm, lo, preferred_element_type=F32))
            cex = cum - lw
            m = cum[mid:mid + 1, :]
            cend = cum[end:end + 1, :]
            e1 = jnp.exp(cex - m)
            e2 = jnp.exp(m - cum)
            e3 = e1 if bwd else jnp.exp(cum - m)
            em = jnp.exp(m)
            eem = jnp.exp(cend - m)
            a_q = -kk * e1
            b_k = b * e2
            k_k = kd * e2
            r_q = r * e3
            slabs.append(dict(sl=sl, a_q=a_q, b_k=b_k, k_k=k_k, r_q=r_q, a_0=a_q * em, r_0=r_q * em,
                              b_e=b_k * eem, k_e=k_k * eem, v=v, we=jnp.exp(cend)))

        def per_chain(fn):
            return [fn(d, g) for d, g in chains]

        sc = per_chain(lambda d, g: _bdot_nt(
            jnp.concatenate([stack(pair(slabs[d]['a_q'], g)), stack(pair(slabs[d]['r_q'], g))], axis=0),
            jnp.concatenate([stack(pair(slabs[d]['b_k'], g)), stack(pair(slabs[d]['k_k'], g))], axis=0)))
        A = [sc[i][:P, :P] * mask_ref[d, 0] for i, (d, g) in enumerate(chains)]
        vst = per_chain(lambda d, g: stack(pair(slabs[d]['v'], g)))
        D = [A[i] * mask_ref[d, 2] for i, (d, g) in enumerate(chains)]
        D2 = [_bdot(x, x) for x in D]
        bv = [_bdot(jnp.concatenate([sc[i][:P, P:] * mask_ref[d, 0], sc[i][P:, P:] * mask_ref[d, 1]], axis=0), vst[i])
              for i, (d, g) in enumerate(chains)]
        T1 = [_bdot(D2[i], D[i]) for i in range(len(chains))]
        D4 = [_bdot(x, x) for x in D2]
        X = [mask_ref[d, 6] + D[i] + D2[i] + T1[i] for i, (d, g) in enumerate(chains)]
        X = [X[i] + _bdot(D4[i], X[i]) for i in range(len(chains))]
        for lvl in (3, 4, 5):
            T = [_bdot(X[i], A[i] * mask_ref[d, lvl]) for i, (d, g) in enumerate(chains)]
            X = [X[i] + _bdot(T[i], X[i]) for i in range(len(chains))]
        pq = [_bdot(X[i], jnp.concatenate([stack(pair(slabs[d]['a_0'], g)), bv[i][:P]], axis=1))
              for i, (d, g) in enumerate(chains)]
        S = [s_ref[d, g] for d, g in chains]
        ps = [_bdot_nt(jnp.concatenate([pq[i][:, :P], stack(pair(slabs[d]['r_0'], g))], axis=0), S[i])
              for i, (d, g) in enumerate(chains)]
        U = [ps[i][:P] + pq[i][:, P:] for i in range(len(chains))]
        MbU = [_bdot(sc[i][P:, :P] * mask_ref[d, 1], U[i]) for i, (d, g) in enumerate(chains)]
        Sn = [_bdot_tn(jnp.concatenate([U[i], vst[i]], axis=0),
                       jnp.concatenate([stack(pair(slabs[d]['b_e'], g)), stack(pair(slabs[d]['k_e'], g))], axis=0))
              for i, (d, g) in enumerate(chains)]
        for i, (d, g) in enumerate(chains):
            Y = ps[i][P:] + MbU[i] + bv[i][P:]
            outs[d][slabs[d]['sl'], g * P:(g + 1) * P] = Y[:CHUNK] + Y[CHUNK:]
            s_ref[d, g] = S[i] * pair(slabs[d]['we'], g) + Sn[i]
        return carry

    lax.fori_loop(0, n_chunks, chunk, 0)


def rwkv_chunks(r, v, kk, lw_f, kd_f, b_f, lw_b, kd_b, b_b, *, T, tt, pg):
    n = r.shape[0]
    n_seq = n // T
    nt = T // tt
    masks, cum = _chunk_masks()
    w = pg * PAIR
    fwd = pl.BlockSpec((tt, w), lambda s, p, j: (s * nt + j, p))
    bwd = pl.BlockSpec((tt, w), lambda s, p, j: (s * nt + nt - 1 - j, p))
    return pl.pallas_call(
        _rwkv_chunks_kernel,
        out_shape=(jax.ShapeDtypeStruct((n, RWKV_WIDTH), F32), jax.ShapeDtypeStruct((n, RWKV_WIDTH), F32)),
        grid=(n_seq, N_PAIRS // pg, nt),
        in_specs=[fwd] * 6 + [bwd] * 6 + [pl.BlockSpec(masks.shape, lambda s, p, j: (0, 0, 0, 0)),
                                          pl.BlockSpec(cum.shape, lambda s, p, j: (0, 0, 0))],
        out_specs=(fwd, bwd),
        scratch_shapes=[pltpu.VMEM((2, pg, PAIR, PAIR), F32)],
        compiler_params=_params(("parallel", "parallel", "arbitrary")),
        name="rwkv_chunks",
    )(r, lw_f, kd_f, v, kk, b_f, r, lw_b, kd_b, v, kk, b_b, masks, cum)


def _rwkv_post_kernel(yf_ref, yb_ref, g_ref, bg_ref, ln_ref, seg_ref, o_ref):
    y = yf_ref[...] + yb_ref[...]
    segm = seg_ref[...]
    d = y - _seg_dot(y, segm)
    var = _seg_dot(d * d, segm)
    yn = d * lax.rsqrt(var + GN_EPS) * ln_ref[0:1, :] + ln_ref[1:2, :]
    o_ref[...] = (yn * g_ref[...] + bg_ref[...]).astype(o_ref.dtype)


def rwkv_post(y_f, y_b, g, bg, ln, *, tm, w):
    n = y_f.shape[0]
    spec = pl.BlockSpec((tm, w), lambda i, p: (i, p))
    return pl.pallas_call(
        _rwkv_post_kernel,
        out_shape=jax.ShapeDtypeStruct((n, RWKV_WIDTH), BF16),
        grid=(n // tm, RWKV_WIDTH // w),
        in_specs=[spec] * 4 + [pl.BlockSpec((2, w), lambda i, p: (0, p)),
                               pl.BlockSpec((PAIR, PAIR), lambda i, p: (0, 0))],
        out_specs=spec,
        compiler_params=_params(("parallel", "parallel")),
        name="rwkv_post",
    )(y_f, y_b, g, bg, ln, _seg_matrix(1.0 / RWKV_HEAD_DIM))


def _pad_cols(w, n):
    return jnp.pad(w, [(0, 0)] * (w.ndim - 1) + [(0, n - w.shape[-1])])


def _pack_lowrank_cols(w, main):
    parts = [w[..., :main]]
    off = main
    for _ in range(4):
        parts.append(_pad_cols(w[..., off:off + DECAY_RANK], RANK_PAD))
        off += DECAY_RANK
    parts.append(w[..., off:])
    return jnp.concatenate(parts, axis=-1)


def _mixer(zin, tabs, lw, T):
    q, k, v = qk_prep(zin, tabs, lw['q_norm'], lw['k_norm'], T=T, tm=256)
    att = attention(q, k, v, T=T, bq=512, kc=2048, rb=128, ahead=2)
    r, vv, kk, lw_f, lw_b, kd_f, kd_b, b_f, b_b, g, bg = rwkv_prep(
        zin, lw['mu'], lw['wup'], lw['aup'], lw['gup'], lw['vecs'], T=T, tm=512, w=512)
    y_f, y_b = rwkv_chunks(r, vv, kk, lw_f, kd_f, b_f, lw_b, kd_b, b_b, T=T, tt=256, pg=8)
    return att, rwkv_post(y_f, y_b, g, bg, lw['ln'], tm=512, w=512)


def _trunk(x3, layers, final_norm):
    B, T, D = x3.shape
    x = x3.reshape(B * T, D)
    tabs = _rope_tables(T)
    for lw in layers:
        act = norm_swiglu(x, lw['ffn1_norm'], lw['g1'], lw['u1'], bm=512, bn=512)
        x = matmul_residual(act, lw['d1'], x, scale=0.5, bm=512, bn=1024)
        zin = norm_matmul(x, lw['mix_norm'], lw['w_in'], bm=512, bn=768, out_dtype=F32)
        att, rw = _mixer(zin, tabs, lw, T)
        x = matmul2_residual(att, rw, lw['w_out'], x, bm=512, bn=1024)
        act = norm_swiglu(x, lw['ffn2_norm'], lw['g2'], lw['u2'], bm=512, bn=512)
        x = matmul_residual(act, lw['d2'], x, scale=0.5, bm=512, bn=1024)
    return rmsnorm(x, final_norm, bm=256).reshape(B, T, D)


def _layer_weights(l, p):
    main = ATT_COLS + 3 * RWKV_WIDTH
    pad_rank = ((0, 0), (0, RANK_PAD - DECAY_RANK), (0, 0))
    zero = jnp.zeros((RWKV_WIDTH,), F32)
    return dict(
        ffn1_norm=p['ffn1_norm'][l], mix_norm=p['mix_norm'][l], ffn2_norm=p['ffn2_norm'][l],
        q_norm=p['q_norm'][l], k_norm=p['k_norm'][l],
        g1=_pad_cols(p['ffn1_gate'][l].astype(BF16), D_FF_PAD),
        u1=_pad_cols(p['ffn1_up'][l].astype(BF16), D_FF_PAD),
        d1=jnp.pad(p['ffn1_down'][l].astype(BF16), ((0, D_FF_PAD - D_FF), (0, 0))),
        g2=_pad_cols(p['ffn2_gate'][l].astype(BF16), D_FF_PAD),
        u2=_pad_cols(p['ffn2_up'][l].astype(BF16), D_FF_PAD),
        d2=jnp.pad(p['ffn2_down'][l].astype(BF16), ((0, D_FF_PAD - D_FF), (0, 0))),
        w_in=_pack_lowrank_cols(p['w_in'][l].astype(BF16), main),
        w_out=p['w_out'][l].astype(BF16),
        mu=_pack_lowrank_cols(p['shift_mu'][l], 3 * RWKV_WIDTH),
        wup=jnp.pad(p['decay_up'][l], pad_rank).astype(BF16),
        aup=jnp.pad(p['icl_up'][l], pad_rank).astype(BF16),
        gup=p['gate_up'][l].astype(BF16),
        vecs=jnp.stack([p['decay_w0'][l, 0], p['decay_w0'][l, 1], p['icl_a0'][l, 0], p['icl_a0'][l, 1],
                        p['rwkv_k_k'][l], p['rwkv_k_a'][l], p['rwkv_r_k'][l].reshape(-1), zero]),
        ln=jnp.stack([p['ln_x_g'][l], p['ln_x_b'][l]]),
    )


def kernel(x_prompt, x_sample, ffn1_norm, ffn1_gate, ffn1_up, ffn1_down, mix_norm, w_in, q_norm, k_norm, shift_mu, decay_w0, decay_up, icl_a0, icl_up, gate_up, rwkv_k_k, rwkv_k_a, rwkv_r_k, ln_x_g, ln_x_b, w_out, ffn2_norm, ffn2_gate, ffn2_up, ffn2_down, final_norm):
    p = dict(ffn1_norm=ffn1_norm, ffn1_gate=ffn1_gate, ffn1_up=ffn1_up, ffn1_down=ffn1_down, mix_norm=mix_norm,
             w_in=w_in, q_norm=q_norm, k_norm=k_norm, shift_mu=shift_mu, decay_w0=decay_w0, decay_up=decay_up,
             icl_a0=icl_a0, icl_up=icl_up, gate_up=gate_up, rwkv_k_k=rwkv_k_k, rwkv_k_a=rwkv_k_a,
             rwkv_r_k=rwkv_r_k, ln_x_g=ln_x_g, ln_x_b=ln_x_b, w_out=w_out, ffn2_norm=ffn2_norm,
             ffn2_gate=ffn2_gate, ffn2_up=ffn2_up, ffn2_down=ffn2_down)
    layers = [_layer_weights(l, p) for l in range(DEPTH)]
    return (_trunk(x_prompt, layers, final_norm), _trunk(x_sample, layers, final_norm))
```

```python
import functools

import jax
import jax.numpy as jnp
import numpy as np
from jax import lax
from jax.experimental import pallas as pl
from jax.experimental.pallas import tpu as pltpu

F32 = jnp.float32
BF16 = jnp.bfloat16

D_MODEL = 4096
DEPTH = 4
GRID_W = 64
ATT_HEAD_DIM = 128
ATT_WIDTH = D_MODEL // 2
ATT_HEADS = ATT_WIDTH // ATT_HEAD_DIM
ATT_KV_HEADS = ATT_HEADS // 4
ATT_KV_WIDTH = ATT_KV_HEADS * ATT_HEAD_DIM
ROPE_BASE = 10000.0
RWKV_WIDTH = D_MODEL - ATT_WIDTH
RWKV_HEAD_DIM = 64
RWKV_HEADS = RWKV_WIDTH // RWKV_HEAD_DIM
DECAY_RANK = 96
ICL_RANK = 96
GATE_RANK = 256
D_FF = 5504
NORM_EPS = 1e-6
GN_EPS = 64e-5
ATT_COLS = ATT_WIDTH + 2 * ATT_KV_WIDTH

LANES = 128
MXU_COLS = 256
VMEM_LIMIT = 56 * 1024 * 1024

D_FF_PAD = -(-D_FF // MXU_COLS) * MXU_COLS
RANK_PAD = LANES
LOWRANK_COLS = 4 * RANK_PAD + GATE_RANK
IN_COLS_PAD = ATT_COLS + 3 * RWKV_WIDTH + LOWRANK_COLS


def _params(sem):
    return pltpu.CompilerParams(dimension_semantics=sem, vmem_limit_bytes=VMEM_LIMIT)


def _rmsnorm_rows(x_ref, g_ref, h_ref, rows):
    bm, k = x_ref.shape

    def body(i, carry):
        rsl = pl.ds(pl.multiple_of(i * rows, rows), rows)
        acc = jnp.zeros((rows, LANES), F32)
        for c in range(k // LANES):
            xc = x_ref[rsl, c * LANES:(c + 1) * LANES]
            acc = acc + xc * xc
        rs = lax.rsqrt(jnp.sum(acc, axis=-1, keepdims=True) * (1.0 / k) + NORM_EPS)
        for c in range(k // LANES):
            sl = slice(c * LANES, (c + 1) * LANES)
            h_ref[rsl, sl] = (x_ref[rsl, sl] * rs * g_ref[:, sl]).astype(BF16)
        return carry

    lax.fori_loop(0, bm // rows, body, 0)


def _norm_matmul_kernel(x_ref, g_ref, w_ref, o_ref, h_ref):
    @pl.when(pl.program_id(1) == 0)
    def _():
        _rmsnorm_rows(x_ref, g_ref, h_ref, 128)

    o_ref[...] = jnp.dot(h_ref[...], w_ref[...], preferred_element_type=F32).astype(o_ref.dtype)


def _norm_swiglu_kernel(x_ref, g_ref, wg_ref, wu_ref, o_ref, h_ref):
    @pl.when(pl.program_id(1) == 0)
    def _():
        _rmsnorm_rows(x_ref, g_ref, h_ref, 128)

    h = h_ref[...]
    gate = jnp.dot(h, wg_ref[...], preferred_element_type=F32)
    up = jnp.dot(h, wu_ref[...], preferred_element_type=F32)
    o_ref[...] = (gate * jax.nn.sigmoid(gate) * up).astype(o_ref.dtype)


def _matmul_residual_kernel(a_ref, w_ref, r_ref, o_ref, *, scale):
    acc = jnp.dot(a_ref[...], w_ref[...], preferred_element_type=F32)
    o_ref[...] = r_ref[...] + scale * acc


def _layer_block(layer, rows, bn):
    return pl.BlockSpec((None, rows, bn), lambda i, j: (layer, 0, j))


def norm_matmul(x, g, w, layer, *, bm, bn, out_dtype):
    m, k = x.shape
    n = w.shape[2]
    return pl.pallas_call(
        _norm_matmul_kernel,
        out_shape=jax.ShapeDtypeStruct((m, n), out_dtype),
        grid=(m // bm, n // bn),
        in_specs=[pl.BlockSpec((bm, k), lambda i, j: (i, 0)),
                  pl.BlockSpec((1, k), lambda i, j: (0, 0)),
                  _layer_block(layer, k, bn)],
        out_specs=pl.BlockSpec((bm, bn), lambda i, j: (i, j)),
        scratch_shapes=[pltpu.VMEM((bm, k), BF16)],
        compiler_params=_params(("parallel", "arbitrary")),
        name="norm_matmul",
    )(x, g.reshape(1, k), w)


def norm_swiglu(x, g, wg, wu, layer, *, bm, bn):
    m, k = x.shape
    n = wg.shape[2]
    return pl.pallas_call(
        _norm_swiglu_kernel,
        out_shape=jax.ShapeDtypeStruct((m, n), BF16),
        grid=(m // bm, n // bn),
        in_specs=[pl.BlockSpec((bm, k), lambda i, j: (i, 0)),
                  pl.BlockSpec((1, k), lambda i, j: (0, 0)),
                  _layer_block(layer, k, bn),
                  _layer_block(layer, k, bn)],
        out_specs=pl.BlockSpec((bm, bn), lambda i, j: (i, j)),
        scratch_shapes=[pltpu.VMEM((bm, k), BF16)],
        compiler_params=_params(("parallel", "arbitrary")),
        name="norm_swiglu",
    )(x, g.reshape(1, k), wg, wu)


def matmul_residual(a, w, layer, res, *, scale, bm, bn):
    m, k = a.shape
    n = w.shape[2]
    return pl.pallas_call(
        functools.partial(_matmul_residual_kernel, scale=scale),
        out_shape=jax.ShapeDtypeStruct((m, n), F32),
        grid=(m // bm, n // bn),
        in_specs=[pl.BlockSpec((bm, k), lambda i, j: (i, 0)),
                  _layer_block(layer, k, bn),
                  pl.BlockSpec((bm, bn), lambda i, j: (i, j))],
        out_specs=pl.BlockSpec((bm, bn), lambda i, j: (i, j)),
        input_output_aliases={2: 0},
        compiler_params=_params(("parallel", "parallel")),
        name="matmul_residual",
    )(a, w, res)


def _matmul2_residual_kernel(a1_ref, a2_ref, w_ref, r_ref, o_ref):
    k1 = a1_ref.shape[1]
    acc = jnp.dot(a1_ref[...], w_ref[:k1, :], preferred_element_type=F32)
    acc = acc + jnp.dot(a2_ref[...], w_ref[k1:, :], preferred_element_type=F32)
    o_ref[...] = r_ref[...] + acc


def matmul2_residual(a1, a2, w, layer, res, *, bm, bn):
    m, k1 = a1.shape
    k2 = a2.shape[1]
    n = w.shape[2]
    return pl.pallas_call(
        _matmul2_residual_kernel,
        out_shape=jax.ShapeDtypeStruct((m, n), F32),
        grid=(m // bm, n // bn),
        in_specs=[pl.BlockSpec((bm, k1), lambda i, j: (i, 0)),
                  pl.BlockSpec((bm, k2), lambda i, j: (i, 0)),
                  _layer_block(layer, k1 + k2, bn),
                  pl.BlockSpec((bm, bn), lambda i, j: (i, j))],
        out_specs=pl.BlockSpec((bm, bn), lambda i, j: (i, j)),
        input_output_aliases={3: 0},
        compiler_params=_params(("parallel", "parallel")),
        name="matmul2_residual",
    )(a1, a2, w, res)


def _rmsnorm_kernel(x_ref, g_ref, o_ref):
    x = x_ref[...]
    ms = jnp.mean(x * x, axis=-1, keepdims=True)
    o_ref[...] = x * lax.rsqrt(ms + NORM_EPS) * g_ref[...]


def rmsnorm(x, g, *, bm):
    m, k = x.shape
    return pl.pallas_call(
        _rmsnorm_kernel,
        out_shape=jax.ShapeDtypeStruct((m, k), F32),
        grid=(m // bm,),
        in_specs=[pl.BlockSpec((bm, k), lambda i: (i, 0)),
                  pl.BlockSpec((1, k), lambda i: (0, 0))],
        out_specs=pl.BlockSpec((bm, k), lambda i: (i, 0)),
        compiler_params=_params(("parallel",)),
        name="final_rmsnorm",
    )(x, g.reshape(1, k))


def _rope_tables(T):
    half = ATT_HEAD_DIM // 2
    quarter = half // 2
    t = jnp.arange(T)
    pos = jnp.stack([(t // GRID_W), (t % GRID_W)], axis=1).astype(F32)
    inv = jnp.power(ROPE_BASE, -jnp.arange(0, half, 2, dtype=F32) / half)
    lane = np.arange(ATT_HEAD_DIM)
    ang = pos[:, lane // half] * inv[lane % quarter][None, :]
    first = jnp.asarray((lane % half) < quarter)[None, :]
    cos, sin = jnp.cos(ang), jnp.sin(ang)
    return cos, jnp.where(first, -sin, 0.0), jnp.where(first, 0.0, sin)


def _qk_prep_kernel(q_ref, k_ref, v_ref, c_ref, sa_ref, sb_ref, qg_ref, kg_ref, qo_ref, ko_ref, vo_ref):
    c, sa, sb = c_ref[...], sa_ref[...], sb_ref[...]
    quarter = ATT_HEAD_DIM // 4

    def prep(x, gain, scale, dtype=BF16):
        ms = jnp.mean(x * x, axis=-1, keepdims=True)
        y = x * lax.rsqrt(ms + NORM_EPS) * gain
        y = y * c + pltpu.roll(y, ATT_HEAD_DIM - quarter, 1) * sa + pltpu.roll(y, quarter, 1) * sb
        return (y * scale).astype(dtype)

    for h in range(ATT_HEADS):
        sl = slice(h * ATT_HEAD_DIM, (h + 1) * ATT_HEAD_DIM)
        qo_ref[:, sl] = prep(q_ref[:, sl], qg_ref[...], ATT_HEAD_DIM ** -0.5)
    for h in range(ATT_KV_HEADS):
        sl = slice(h * ATT_HEAD_DIM, (h + 1) * ATT_HEAD_DIM)
        ko_ref[sl, :] = prep(k_ref[:, sl], kg_ref[...], 1.0, F32).T.astype(BF16)
    vo_ref[...] = v_ref[...].astype(BF16)


def qk_prep(zin, tabs, q_gain, k_gain, *, T, tm):
    n = zin.shape[0]
    tps = T // tm
    kvb = ATT_WIDTH // ATT_KV_WIDTH
    tab_spec = pl.BlockSpec((tm, ATT_HEAD_DIM), lambda i: (i % tps, 0))
    vec_spec = pl.BlockSpec((1, ATT_HEAD_DIM), lambda i: (0, 0))
    return pl.pallas_call(
        _qk_prep_kernel,
        out_shape=(jax.ShapeDtypeStruct((n, ATT_WIDTH), BF16),
                   jax.ShapeDtypeStruct((ATT_KV_WIDTH, n), BF16),
                   jax.ShapeDtypeStruct((n, ATT_KV_WIDTH), BF16)),
        grid=(n // tm,),
        in_specs=[pl.BlockSpec((tm, ATT_WIDTH), lambda i: (i, 0)),
                  pl.BlockSpec((tm, ATT_KV_WIDTH), lambda i: (i, kvb)),
                  pl.BlockSpec((tm, ATT_KV_WIDTH), lambda i: (i, kvb + 1)),
                  tab_spec, tab_spec, tab_spec, vec_spec, vec_spec],
        out_specs=(pl.BlockSpec((tm, ATT_WIDTH), lambda i: (i, 0)),
                   pl.BlockSpec((ATT_KV_WIDTH, tm), lambda i: (0, i)),
                   pl.BlockSpec((tm, ATT_KV_WIDTH), lambda i: (i, 0))),
        compiler_params=_params(("parallel",)),
        name="qk_prep",
    )(zin, zin, zin, *tabs, q_gain.reshape(1, -1), k_gain.reshape(1, -1))


def _attn_kernel(q_ref, kt_ref, v_ref, o_ref, *scratch, kc, rb, ahead):
    bq = q_ref.shape[0]
    T = v_ref.shape[0]
    G = ATT_HEADS // ATT_KV_HEADS
    hd = ATT_HEAD_DIM
    nb = G * bq // rb
    m_refs, l_refs, acc_refs = scratch[:nb], scratch[nb:2 * nb], scratch[2 * nb:]
    for i in range(nb):
        m_refs[i][...] = jnp.full(m_refs[i].shape, -jnp.inf, F32)
        l_refs[i][...] = jnp.zeros(l_refs[i].shape, F32)
        acc_refs[i][...] = jnp.zeros(acc_refs[i].shape, F32)

    def body(c, carry):
        c0 = pl.multiple_of(c * kc, kc)
        kt = kt_ref[:, pl.ds(c0, kc)]
        vch = v_ref[pl.ds(c0, kc), :]

        def scores(i):
            g, r = divmod(i, bq // rb)
            return jnp.dot(q_ref[r * rb:(r + 1) * rb, g * hd:(g + 1) * hd], kt, preferred_element_type=F32)

        pending = [scores(i) for i in range(min(ahead, nb))]
        for i in range(nb):
            s = pending.pop(0)
            if i + ahead < nb:
                pending.append(scores(i + ahead))
            m_prev = m_refs[i][...]
            m_new = jnp.maximum(m_prev, jnp.max(s, axis=-1, keepdims=True))
            alpha = jnp.exp(m_prev - m_new)
            p = jnp.exp(s - m_new)
            psum = p[:, :LANES]
            for j in range(1, kc // LANES):
                psum = psum + p[:, j * LANES:(j + 1) * LANES]
            l_refs[i][...] = alpha * l_refs[i][...] + psum
            acc_refs[i][...] = alpha * acc_refs[i][...] + jnp.dot(p.astype(BF16), vch, preferred_element_type=F32)
            m_refs[i][...] = m_new
        return carry

    lax.fori_loop(0, T // kc, body, 0)
    for i in range(nb):
        g, r = divmod(i, bq // rb)
        l = jnp.sum(l_refs[i][...], axis=-1, keepdims=True)
        o_ref[r * rb:(r + 1) * rb, g * hd:(g + 1) * hd] = (acc_refs[i][...] / l).astype(o_ref.dtype)


def attention(q, kt, v, *, T, bq, kc, rb, ahead):
    n = q.shape[0]
    n_seq = n // T
    G = ATT_HEADS // ATT_KV_HEADS
    gw = G * ATT_HEAD_DIM
    nq = T // bq
    nb = G * bq // rb
    return pl.pallas_call(
        functools.partial(_attn_kernel, kc=kc, rb=rb, ahead=ahead),
        out_shape=jax.ShapeDtypeStruct((n, ATT_WIDTH), BF16),
        grid=(n_seq, ATT_KV_HEADS, nq),
        in_specs=[pl.BlockSpec((bq, gw), lambda s, h, i: (s * nq + i, h)),
                  pl.BlockSpec((ATT_HEAD_DIM, T), lambda s, h, i: (h, s)),
                  pl.BlockSpec((T, ATT_HEAD_DIM), lambda s, h, i: (s, h))],
        out_specs=pl.BlockSpec((bq, gw), lambda s, h, i: (s * nq + i, h)),
        scratch_shapes=([pltpu.VMEM((rb, 1), F32)] * nb + [pltpu.VMEM((rb, LANES), F32)] * nb
                        + [pltpu.VMEM((rb, ATT_HEAD_DIM), F32)] * nb),
        compiler_params=_params(("parallel", "parallel", "arbitrary")),
        name="attention",
    )(q, kt, v)


PAIR = 2 * RWKV_HEAD_DIM
N_PAIRS = RWKV_WIDTH // PAIR
CHUNK = 64
DECAY_SCALE = float(np.exp(-0.5))


def _seg_matrix(value=1.0):
    lane = np.arange(PAIR)
    same = lane[:, None] // RWKV_HEAD_DIM == lane[None, :] // RWKV_HEAD_DIM
    return jnp.asarray(same.astype(np.float32) * value).astype(BF16)


def _split3(x):
    hi = x.astype(BF16)
    r1 = x - hi.astype(F32)
    md = r1.astype(BF16)
    return hi, md, (r1 - md.astype(F32)).astype(BF16)


def _seg_dot(x, seg):
    out = []
    for q in range(x.shape[1] // PAIR):
        hi, md, lo = _split3(x[:, q * PAIR:(q + 1) * PAIR])
        out.append(jnp.dot(hi, seg, preferred_element_type=F32) + jnp.dot(md, seg, preferred_element_type=F32)
                   + jnp.dot(lo, seg, preferred_element_type=F32))
    return jnp.concatenate(out, axis=1)


def _rwkv_prep_kernel(zr_ref, zrp_ref, zrn_ref, zk_ref, zkp_ref, zkn_ref, zv_ref, zvp_ref, zvn_ref,
                      zl_ref, zlp_ref, zln_ref, mur_ref, muk_ref, muv_ref, mul_ref,
                      wup_ref, aup_ref, gup_ref, vec_ref, seg_ref,
                      r_o, v_o, kk_o, lwf_o, lwb_o, kdf_o, kdb_o, bf_o, bb_o, g_o, bg_o,
                      lr_ref, *, tiles_per_seq):
    i = pl.program_id(0)
    tm = zr_ref.shape[0]
    first = (i % tiles_per_seq) == 0
    last = (i % tiles_per_seq) == tiles_per_seq - 1
    rows = lax.broadcasted_iota(jnp.int32, (tm, 1), 0)

    def shift(z_ref, zp_ref, zn_ref, mu_ref):
        z = z_ref[...]
        prev_row = jnp.where(first, 0.0, zp_ref[7:8, :])
        next_row = jnp.where(last, 0.0, zn_ref[0:1, :])
        zp = jnp.where(rows == 0, prev_row, pltpu.roll(z, 1, 0))
        zn = jnp.where(rows == tm - 1, next_row, pltpu.roll(z, tm - 1, 0))
        mu_p, mu_n = mu_ref[0:1, :], mu_ref[1:2, :]
        return (1.0 - mu_p - mu_n) * z + mu_p * zp + mu_n * zn

    @pl.when(pl.program_id(1) == 0)
    def _():
        lr = shift(zl_ref, zlp_ref, zln_ref, mul_ref)
        nd = 2 * RANK_PAD
        lr_ref[:, :nd] = jnp.tanh(lr[:, :nd]).astype(BF16)
        lr_ref[:, nd:2 * nd] = lr[:, nd:2 * nd].astype(BF16)
        lr_ref[:, 2 * nd:] = jax.nn.sigmoid(lr[:, 2 * nd:]).astype(BF16)

    r = shift(zr_ref, zrp_ref, zrn_ref, mur_ref)
    k = shift(zk_ref, zkp_ref, zkn_ref, muk_ref)
    v = shift(zv_ref, zvp_ref, zvn_ref, muv_ref)
    seg = seg_ref[...]
    k_k, k_a, r_k = vec_ref[4:5, :], vec_ref[5:6, :], vec_ref[6:7, :]

    kk = k * k_k
    kk = kk * lax.rsqrt(jnp.maximum(_seg_dot(kk * kk, seg), 1e-24))

    def direction(d):
        tw = lr_ref[:, d * RANK_PAD:(d + 1) * RANK_PAD]
        ad = lr_ref[:, (2 + d) * RANK_PAD:(3 + d) * RANK_PAD]
        u = vec_ref[d:d + 1, :] + jnp.dot(tw, wup_ref[d], preferred_element_type=F32)
        lw = -DECAY_SCALE * jax.nn.sigmoid(u)
        a = jax.nn.sigmoid(vec_ref[2 + d:3 + d, :] + jnp.dot(ad, aup_ref[d], preferred_element_type=F32))
        kd = k * (1.0 + (a - 1.0) * k_a)
        return lw, kd, kk * a

    lw_f, kd_f, b_f = direction(0)
    lw_b, kd_b, b_b = direction(1)
    g = jnp.dot(lr_ref[:, 4 * RANK_PAD:], gup_ref[...], preferred_element_type=F32)
    bonus = _seg_dot(r * kd_f * r_k, seg) * v
    r_o[...] = r
    v_o[...] = v
    kk_o[...] = kk
    lwf_o[...] = lw_f
    lwb_o[...] = lw_b
    kdf_o[...] = kd_f
    kdb_o[...] = kd_b
    bf_o[...] = b_f
    bb_o[...] = b_b
    g_o[...] = g
    bg_o[...] = bonus * g


def rwkv_prep(zin, mu, wup, aup, gup, vecs, *, T, tm, w):
    n = zin.shape[0]
    tps = T // tm
    hb = tm // 8
    nb8 = n // 8
    nw = RWKV_WIDTH // w
    c_r = ATT_COLS // w
    c_k = c_r + nw
    c_v = c_k + nw
    c_l = (ATT_COLS + 3 * RWKV_WIDTH) // LOWRANK_COLS
    assert c_r * w == ATT_COLS and c_l * LOWRANK_COLS == ATT_COLS + 3 * RWKV_WIDTH

    def main_specs(c0):
        return [pl.BlockSpec((tm, w), lambda i, p: (i, c0 + p)),
                pl.BlockSpec((8, w), lambda i, p: (jnp.maximum(i * hb - 1, 0), c0 + p)),
                pl.BlockSpec((8, w), lambda i, p: (jnp.minimum((i + 1) * hb, nb8 - 1), c0 + p))]

    low_specs = [pl.BlockSpec((tm, LOWRANK_COLS), lambda i, p: (i, c_l)),
                 pl.BlockSpec((8, LOWRANK_COLS), lambda i, p: (jnp.maximum(i * hb - 1, 0), c_l)),
                 pl.BlockSpec((8, LOWRANK_COLS), lambda i, p: (jnp.minimum((i + 1) * hb, nb8 - 1), c_l))]
    mu_specs = [pl.BlockSpec((2, w), lambda i, p: (0, p)),
                pl.BlockSpec((2, w), lambda i, p: (0, nw + p)),
                pl.BlockSpec((2, w), lambda i, p: (0, 2 * nw + p)),
                pl.BlockSpec((2, LOWRANK_COLS), lambda i, p: (0, 3 * RWKV_WIDTH // LOWRANK_COLS))]
    w_specs = [pl.BlockSpec((2, RANK_PAD, w), lambda i, p: (0, 0, p)),
               pl.BlockSpec((2, RANK_PAD, w), lambda i, p: (0, 0, p)),
               pl.BlockSpec((GATE_RANK, w), lambda i, p: (0, p)),
               pl.BlockSpec((8, w), lambda i, p: (0, p)),
               pl.BlockSpec((PAIR, PAIR), lambda i, p: (0, 0))]
    out_spec = pl.BlockSpec((tm, w), lambda i, p: (i, p))
    n_out = 11
    return pl.pallas_call(
        functools.partial(_rwkv_prep_kernel, tiles_per_seq=tps),
        out_shape=tuple(jax.ShapeDtypeStruct((n, RWKV_WIDTH), F32) for _ in range(n_out)),
        grid=(n // tm, nw),
        in_specs=main_specs(c_r) + main_specs(c_k) + main_specs(c_v) + low_specs + mu_specs + w_specs,
        out_specs=tuple(out_spec for _ in range(n_out)),
        scratch_shapes=[pltpu.VMEM((tm, LOWRANK_COLS), BF16)],
        compiler_params=_params(("parallel", "arbitrary")),
        name="rwkv_prep",
    )(*([zin] * 12), mu, mu, mu, mu, wup, aup, gup, vecs, _seg_matrix())


def _chunk_masks():
    idx = np.arange(PAIR)
    h, t = idx // CHUNK, idx % CHUNK
    same = h[:, None] == h[None, :]
    tc = np.arange(CHUNK)

    def bd(w):
        return same & (t[:, None] // w == t[None, :] // w)

    lv = [bd(8), bd(16) & ~bd(8), bd(32) & ~bd(16), same & ~bd(32)]
    masks, cums = [], []
    for bwd in (False, True):
        tt, ss = (t[None, :], t[:, None]) if bwd else (t[:, None], t[None, :])
        strict = same & (tt > ss)
        read = strict if bwd else same & (tt >= ss)
        masks.append(np.stack([strict, read] + lv + [np.eye(PAIR, dtype=bool)]))
        cums.append((tc[:, None] <= tc[None, :]) if bwd else (tc[:, None] >= tc[None, :]))
    return jnp.asarray(np.stack(masks).astype(np.float32)), jnp.asarray(np.stack(cums).astype(np.float32)).astype(BF16)


def _bdot(a, b):
    return jnp.dot(a.astype(BF16), b.astype(BF16), preferred_element_type=F32)


def _bdot_nt(a, b):
    return lax.dot_general(a.astype(BF16), b.astype(BF16), (((1,), (1,)), ((), ())), preferred_element_type=F32)


def _bdot_tn(a, b):
    return lax.dot_general(a.astype(BF16), b.astype(BF16), (((0,), (0,)), ((), ())), preferred_element_type=F32)


def _rwkv_chunks_kernel(rf_ref, lwf_ref, kdf_ref, vf_ref, kkf_ref, bf_ref,
                        rb_ref, lwb_ref, kdb_ref, vb_ref, kkb_ref, bb_ref,
                        mask_ref, cum_ref, yf_ref, yb_ref, s_ref):
    n_chunks = rf_ref.shape[0] // CHUNK
    pg = rf_ref.shape[1] // PAIR
    P = PAIR
    ins = ((rf_ref, lwf_ref, kdf_ref, vf_ref, kkf_ref, bf_ref), (rb_ref, lwb_ref, kdb_ref, vb_ref, kkb_ref, bb_ref))
    outs = (yf_ref, yb_ref)
    chains = [(d, g) for d in range(2) for g in range(pg)]

    @pl.when(pl.program_id(2) == 0)
    def _():
        s_ref[...] = jnp.zeros(s_ref.shape, F32)

    head_a = lax.broadcasted_iota(jnp.int32, (1, P), 1) < RWKV_HEAD_DIM

    def stack(x):
        return jnp.concatenate([jnp.where(head_a, x, 0.0), jnp.where(head_a, 0.0, x)], axis=0)

    def pair(x, g):
        return x[:, g * P:(g + 1) * P]

    def chunk(ci, carry):
        slabs = []
        for d in range(2):
            bwd = d == 1
            c = (n_chunks - 1 - ci) if bwd else ci
            sl = pl.ds(pl.multiple_of(c * CHUNK, CHUNK), CHUNK)
            r, lw, kd, v, kk, b = (ref[sl, :] for ref in ins[d])
            mid = CHUNK // 2 if bwd else CHUNK // 2 - 1
            end = 0 if bwd else CHUNK - 1
            hi, md, lo = _split3(lw)
            cm = cum_ref[d]
            cum = (jnp.dot(cm, hi, preferred_element_type=F32) + jnp.dot(cm, md, preferred_element_type=F32)
                   + jnp.dot(cm, lo, preferred_element_type=F32))
            cex = cum - lw
            m = cum[mid:mid + 1, :]
            cend = cum[end:end + 1, :]
            e1 = jnp.exp(cex - m)
            e2 = jnp.exp(m - cum)
            e3 = e1 if bwd else jnp.exp(cum - m)
            em = jnp.exp(m)
            eem = jnp.exp(cend - m)
            a_q = -kk * e1
            b_k = b * e2
            k_k = kd * e2
            r_q = r * e3
            slabs.append(dict(sl=sl, a_q=a_q, b_k=b_k, k_k=k_k, r_q=r_q, a_0=a_q * em, r_0=r_q * em,
                              b_e=b_k * eem, k_e=k_k * eem, v=v, we=jnp.exp(cend)))

        def per_chain(fn):
            return [fn(d, g) for d, g in chains]

        sc = per_chain(lambda d, g: _bdot_nt(
            jnp.concatenate([stack(pair(slabs[d]['a_q'], g)), stack(pair(slabs[d]['r_q'], g))], axis=0),
            jnp.concatenate([stack(pair(slabs[d]['b_k'], g)), stack(pair(slabs[d]['k_k'], g))], axis=0)))
        A = [sc[i][:P, :P] * mask_ref[d, 0] for i, (d, g) in enumerate(chains)]
        vst = per_chain(lambda d, g: stack(pair(slabs[d]['v'], g)))
        D = [A[i] * mask_ref[d, 2] for i, (d, g) in enumerate(chains)]
        D2 = [_bdot(x, x) for x in D]
        bv = [_bdot(jnp.concatenate([sc[i][:P, P:] * mask_ref[d, 0], sc[i][P:, P:] * mask_ref[d, 1]], axis=0), vst[i])
              for i, (d, g) in enumerate(chains)]
        T1 = [_bdot(D2[i], D[i]) for i in range(len(chains))]
        D4 = [_bdot(x, x) for x in D2]
        X = [mask_ref[d, 6] + D[i] + D2[i] + T1[i] for i, (d, g) in enumerate(chains)]
        X = [X[i] + _bdot(D4[i], X[i]) for i in range(len(chains))]
        for lvl in (3, 4, 5):
            T = [_bdot(X[i], A[i] * mask_ref[d, lvl]) for i, (d, g) in enumerate(chains)]
            X = [X[i] + _bdot(T[i], X[i]) for i in range(len(chains))]
        pq = [_bdot(X[i], jnp.concatenate([stack(pair(slabs[d]['a_0'], g)), bv[i][:P]], axis=1))
              for i, (d, g) in enumerate(chains)]
        S = [s_ref[d, g] for d, g in chains]
        ps = [_bdot_nt(jnp.concatenate([pq[i][:, :P], stack(pair(slabs[d]['r_0'], g))], axis=0), S[i])
              for i, (d, g) in enumerate(chains)]
        U = [ps[i][:P] + pq[i][:, P:] for i in range(len(chains))]
        MbU = [_bdot(sc[i][P:, :P] * mask_ref[d, 1], U[i]) for i, (d, g) in enumerate(chains)]
        Sn = [_bdot_tn(jnp.concatenate([U[i], vst[i]], axis=0),
                       jnp.concatenate([stack(pair(slabs[d]['b_e'], g)), stack(pair(slabs[d]['k_e'], g))], axis=0))
              for i, (d, g) in enumerate(chains)]
        for i, (d, g) in enumerate(chains):
            Y = ps[i][P:] + MbU[i] + bv[i][P:]
            outs[d][slabs[d]['sl'], g * P:(g + 1) * P] = Y[:CHUNK] + Y[CHUNK:]
            s_ref[d, g] = S[i] * pair(slabs[d]['we'], g) + Sn[i]
        return carry

    lax.fori_loop(0, n_chunks, chunk, 0)


def rwkv_chunks(r, v, kk, lw_f, kd_f, b_f, lw_b, kd_b, b_b, *, T, tt, pg):
    n = r.shape[0]
    n_seq = n // T
    nt = T // tt
    masks, cum = _chunk_masks()
    w = pg * PAIR
    fwd = pl.BlockSpec((tt, w), lambda s, p, j: (s * nt + j, p))
    bwd = pl.BlockSpec((tt, w), lambda s, p, j: (s * nt + nt - 1 - j, p))
    return pl.pallas_call(
        _rwkv_chunks_kernel,
        out_shape=(jax.ShapeDtypeStruct((n, RWKV_WIDTH), F32), jax.ShapeDtypeStruct((n, RWKV_WIDTH), F32)),
        grid=(n_seq, N_PAIRS // pg, nt),
        in_specs=[fwd] * 6 + [bwd] * 6 + [pl.BlockSpec(masks.shape, lambda s, p, j: (0, 0, 0, 0)),
                                          pl.BlockSpec(cum.shape, lambda s, p, j: (0, 0, 0))],
        out_specs=(fwd, bwd),
        scratch_shapes=[pltpu.VMEM((2, pg, PAIR, PAIR), F32)],
        compiler_params=_params(("parallel", "parallel", "arbitrary")),
        name="rwkv_chunks",
    )(r, lw_f, kd_f, v, kk, b_f, r, lw_b, kd_b, v, kk, b_b, masks, cum)


def _rwkv_post_kernel(yf_ref, yb_ref, g_ref, bg_ref, ln_ref, seg_ref, o_ref):
    y = yf_ref[...] + yb_ref[...]
    segm = seg_ref[...]
    d = y - _seg_dot(y, segm)
    var = _seg_dot(d * d, segm)
    yn = d * lax.rsqrt(var + GN_EPS) * ln_ref[0:1, :] + ln_ref[1:2, :]
    o_ref[...] = (yn * g_ref[...] + bg_ref[...]).astype(o_ref.dtype)


def rwkv_post(y_f, y_b, g, bg, ln, *, tm, w):
    n = y_f.shape[0]
    spec = pl.BlockSpec((tm, w), lambda i, p: (i, p))
    return pl.pallas_call(
        _rwkv_post_kernel,
        out_shape=jax.ShapeDtypeStruct((n, RWKV_WIDTH), BF16),
        grid=(n // tm, RWKV_WIDTH // w),
        in_specs=[spec] * 4 + [pl.BlockSpec((2, w), lambda i, p: (0, p)),
                               pl.BlockSpec((PAIR, PAIR), lambda i, p: (0, 0))],
        out_specs=spec,
        compiler_params=_params(("parallel", "parallel")),
        name="rwkv_post",
    )(y_f, y_b, g, bg, ln, _seg_matrix(1.0 / RWKV_HEAD_DIM))


def _pad_cols(w, n):
    return jnp.pad(w, [(0, 0)] * (w.ndim - 1) + [(0, n - w.shape[-1])])


def _pack_lowrank_cols(w, main):
    parts = [w[..., :main]]
    off = main
    for _ in range(4):
        parts.append(_pad_cols(w[..., off:off + DECAY_RANK], RANK_PAD))
        off += DECAY_RANK
    parts.append(w[..., off:])
    return jnp.concatenate(parts, axis=-1)


def _mixer(zin, tabs, lw, T):
    q, k, v = qk_prep(zin, tabs, lw['q_norm'], lw['k_norm'], T=T, tm=256)
    att = attention(q, k, v, T=T, bq=512, kc=2048, rb=128, ahead=2)
    r, vv, kk, lw_f, lw_b, kd_f, kd_b, b_f, b_b, g, bg = rwkv_prep(
        zin, lw['mu'], lw['wup'], lw['aup'], lw['gup'], lw['vecs'], T=T, tm=512, w=512)
    y_f, y_b = rwkv_chunks(r, vv, kk, lw_f, kd_f, b_f, lw_b, kd_b, b_b, T=T, tt=256, pg=8)
    return att, rwkv_post(y_f, y_b, g, bg, lw['ln'], tm=512, w=512)


def _trunk(x3, big, layers, final_norm):
    B, T, D = x3.shape
    x = x3.reshape(B * T, D)
    tabs = _rope_tables(T)
    for l, lw in enumerate(layers):
        act = norm_swiglu(x, lw['ffn1_norm'], big['g1'], big['u1'], l, bm=512, bn=512)
        x = matmul_residual(act, big['d1'], l, x, scale=0.5, bm=512, bn=1024)
        zin = norm_matmul(x, lw['mix_norm'], big['w_in'], l, bm=512, bn=768, out_dtype=F32)
        att, rw = _mixer(zin, tabs, lw, T)
        x = matmul2_residual(att, rw, big['w_out'], l, x, bm=512, bn=1024)
        act = norm_swiglu(x, lw['ffn2_norm'], big['g2'], big['u2'], l, bm=512, bn=512)
        x = matmul_residual(act, big['d2'], l, x, scale=0.5, bm=512, bn=1024)
    return rmsnorm(x, final_norm, bm=256).reshape(B, T, D)


def _big_weights(p):
    main = ATT_COLS + 3 * RWKV_WIDTH
    pad_rows = ((0, 0), (0, D_FF_PAD - D_FF), (0, 0))
    return dict(
        g1=_pad_cols(p['ffn1_gate'].astype(BF16), D_FF_PAD),
        u1=_pad_cols(p['ffn1_up'].astype(BF16), D_FF_PAD),
        d1=jnp.pad(p['ffn1_down'].astype(BF16), pad_rows),
        g2=_pad_cols(p['ffn2_gate'].astype(BF16), D_FF_PAD),
        u2=_pad_cols(p['ffn2_up'].astype(BF16), D_FF_PAD),
        d2=jnp.pad(p['ffn2_down'].astype(BF16), pad_rows),
        w_in=_pack_lowrank_cols(p['w_in'].astype(BF16), main),
        w_out=p['w_out'].astype(BF16),
    )


def _layer_weights(l, p):
    pad_rank = ((0, 0), (0, RANK_PAD - DECAY_RANK), (0, 0))
    zero = jnp.zeros((RWKV_WIDTH,), F32)
    return dict(
        ffn1_norm=p['ffn1_norm'][l], mix_norm=p['mix_norm'][l], ffn2_norm=p['ffn2_norm'][l],
        q_norm=p['q_norm'][l], k_norm=p['k_norm'][l],
        mu=_pack_lowrank_cols(p['shift_mu'][l], 3 * RWKV_WIDTH),
        wup=jnp.pad(p['decay_up'][l], pad_rank).astype(BF16),
        aup=jnp.pad(p['icl_up'][l], pad_rank).astype(BF16),
        gup=p['gate_up'][l].astype(BF16),
        vecs=jnp.stack([p['decay_w0'][l, 0], p['decay_w0'][l, 1], p['icl_a0'][l, 0], p['icl_a0'][l, 1],
                        p['rwkv_k_k'][l], p['rwkv_k_a'][l], p['rwkv_r_k'][l].reshape(-1), zero]),
        ln=jnp.stack([p['ln_x_g'][l], p['ln_x_b'][l]]),
    )


def kernel(x_prompt, x_sample, ffn1_norm, ffn1_gate, ffn1_up, ffn1_down, mix_norm, w_in, q_norm, k_norm, shift_mu, decay_w0, decay_up, icl_a0, icl_up, gate_up, rwkv_k_k, rwkv_k_a, rwkv_r_k, ln_x_g, ln_x_b, w_out, ffn2_norm, ffn2_gate, ffn2_up, ffn2_down, final_norm):
    p = dict(ffn1_norm=ffn1_norm, ffn1_gate=ffn1_gate, ffn1_up=ffn1_up, ffn1_down=ffn1_down, mix_norm=mix_norm,
             w_in=w_in, q_norm=q_norm, k_norm=k_norm, shift_mu=shift_mu, decay_w0=decay_w0, decay_up=decay_up,
             icl_a0=icl_a0, icl_up=icl_up, gate_up=gate_up, rwkv_k_k=rwkv_k_k, rwkv_k_a=rwkv_k_a,
             rwkv_r_k=rwkv_r_k, ln_x_g=ln_x_g, ln_x_b=ln_x_b, w_out=w_out, ffn2_norm=ffn2_norm,
             ffn2_gate=ffn2_gate, ffn2_up=ffn2_up, ffn2_down=ffn2_down)
    big = _big_weights(p)
    layers = [_layer_weights(l, p) for l in range(DEPTH)]
    return (_trunk(x_prompt, big, layers, final_norm), _trunk(x_sample, big, layers, final_norm))
```
